```python
import math
import jax, jax.numpy as jnp
from jax import lax
import numpy as np

D_MODEL = 1024
BATCH = 2
SEQ = 16384
DEPTH = 1
DEC_BATCH = 8
DEC_SEQ = 32
PAST_LEN = 1024

CHUNK = 64
Q_BLOCK = 128
POOL_WINDOWS = (2, 4, 8, 16)
POOL_GROUPS = len(POOL_WINDOWS)
POOL_WIDTH = D_MODEL // 4
POOL_GROUP_DIM = POOL_WIDTH // POOL_GROUPS
POOL_HIST = max(POOL_WINDOWS) - 1
ATTN_WIDTH = D_MODEL - POOL_WIDTH
HEAD_DIM = 64
V_DIM = 2 * HEAD_DIM
N_HEADS = ATTN_WIDTH // V_DIM
MIX_WIDTH = POOL_WIDTH + ATTN_WIDTH
IN_WIDTH = 2 * POOL_WIDTH + 4 * ATTN_WIDTH
IN_SPLITS = (POOL_WIDTH, 2 * POOL_WIDTH, 2 * POOL_WIDTH + ATTN_WIDTH,
             2 * POOL_WIDTH + 2 * ATTN_WIDTH, 2 * POOL_WIDTH + 3 * ATTN_WIDTH)
ATTN_SCALE = HEAD_DIM ** -0.5
DEEPNORM_ALPHA = (2.0 * DEPTH) ** 0.25
DEEPNORM_BETA = (8.0 * DEPTH) ** -0.25
LN_EPS = 1e-5
SUBLN_EPS = 1e-5

kernel_name = "hybrid_pool_diffattn_streaming_step"


def _lambda_init(layer_idx):
    return 0.8 - 0.6 * math.exp(-0.3 * layer_idx)


def _pool_mixer(u, hist, n_valid_hist, pool_w, pool_b, pool_scale):
    f32 = jnp.float32
    B, T, P = u.shape
    ext = jnp.concatenate([hist.astype(f32), u.astype(f32)], axis=1)
    cs = jnp.concatenate([jnp.zeros((B, 1, P), f32), jnp.cumsum(ext, axis=1)], axis=1)
    end = cs[:, POOL_HIST + 1:POOL_HIST + 1 + T]
    t = jnp.arange(T)
    outs = []
    for g, w in enumerate(POOL_WINDOWS):
        lo, hi = g * POOL_GROUP_DIM, (g + 1) * POOL_GROUP_DIM
        start = cs[:, POOL_HIST + 1 - w:POOL_HIST + 1 - w + T, lo:hi]
        cnt = jnp.minimum(w, n_valid_hist + t + 1).astype(f32)
        outs.append((end[..., lo:hi] - start) / cnt[None, :, None])
    pooled = jnp.concatenate(outs, axis=-1) - u.astype(f32)
    pg = pooled.reshape(B, T, POOL_GROUPS, POOL_GROUP_DIM)
    mixed = jnp.einsum("btgc,gcd->btgd", pg, pool_w.astype(f32)) + pool_b.astype(f32)
    return mixed.reshape(B, T, P) * pool_scale.astype(f32)


def _diff_attend(q, k, v, q_pos, k_pos, lam, lam_init, subln_g):
    f32 = jnp.float32
    s = jnp.einsum("bqhcd,bkhcd->bhcqk", q.astype(f32), k.astype(f32)) * ATTN_SCALE
    visible = (k_pos[None, :] // CHUNK) <= (q_pos[:, None] // CHUNK)
    s = jnp.where(visible[None, None, None], s, -jnp.inf)
    p = jax.nn.softmax(s, axis=-1)
    a = p[:, :, 0] - lam * p[:, :, 1]
    o = jnp.einsum("bhqk,bkhv->bqhv", a, v.astype(f32))
    o = o * lax.rsqrt(jnp.mean(o * o, axis=-1, keepdims=True) + SUBLN_EPS)
    return o * subln_g.astype(f32) * (1.0 - lam_init)


def _prompt_attention(q, k, v, lam, lam_init, subln_g):
    B, S = q.shape[:2]
    nblk = S // Q_BLOCK
    qb = q.reshape(B, nblk, Q_BLOCK, N_HEADS, 2, HEAD_DIM).transpose(1, 0, 2, 3, 4, 5)
    k_pos = jnp.arange(S)

    def one_block(args):
        q_blk, i = args
        q_pos = i * Q_BLOCK + jnp.arange(Q_BLOCK)
        return _diff_attend(q_blk, k, v, q_pos, k_pos, lam, lam_init, subln_g)

    o = lax.map(one_block, (qb, jnp.arange(nblk)))
    return o.transpose(1, 0, 2, 3, 4).reshape(B, S, N_HEADS, V_DIM)


def _layer(x, pool_hist, n_valid_hist, k_past, v_past, blocked, lam_init,
           w_in, pool_w, pool_b, pool_scale, lq1, lk1, lq2, lk2, subln_g, w_out, ln_g, ln_b):
    f32 = jnp.float32
    B, T, _ = x.shape
    proj = jnp.einsum("btd,de->bte", x, w_in)
    u, g_pool, q, k, v, g_attn = jnp.split(proj, IN_SPLITS, axis=-1)

    pool_out = _pool_mixer(u, pool_hist, n_valid_hist, pool_w, pool_b, pool_scale)

    q = q.reshape(B, T, N_HEADS, 2, HEAD_DIM)
    k_rows = k.reshape(B, T, N_HEADS, 2 * HEAD_DIM)
    v_rows = v.reshape(B, T, N_HEADS, V_DIM)
    lam = (jnp.exp(jnp.sum(lq1.astype(f32) * lk1.astype(f32)))
           - jnp.exp(jnp.sum(lq2.astype(f32) * lk2.astype(f32))) + lam_init)
    if blocked:
        attn = _prompt_attention(q, k_rows.reshape(B, T, N_HEADS, 2, HEAD_DIM), v_rows,
                                 lam, lam_init, subln_g)
    else:
        pos0 = k_past.shape[1]
        k_all = jnp.concatenate([k_past.astype(k_rows.dtype), k_rows], axis=1)
        v_all = jnp.concatenate([v_past.astype(v_rows.dtype), v_rows], axis=1)
        attn = _diff_attend(q, k_all.reshape(B, pos0 + T, N_HEADS, 2, HEAD_DIM), v_all,
                            pos0 + jnp.arange(T), jnp.arange(pos0 + T), lam, lam_init, subln_g)

    mixed = jnp.concatenate([jax.nn.silu(g_pool.astype(f32)) * pool_out,
                             jax.nn.silu(g_attn.astype(f32)) * attn.reshape(B, T, ATTN_WIDTH)],
                            axis=-1)
    out = jnp.einsum("bte,ed->btd", mixed, w_out.astype(f32))

    z = DEEPNORM_ALPHA * x.astype(f32) + out
    mu = jnp.mean(z, axis=-1, keepdims=True)
    var = jnp.mean(jnp.square(z - mu), axis=-1, keepdims=True)
    y = (z - mu) * lax.rsqrt(var + LN_EPS) * ln_g.astype(f32) + ln_b.astype(f32)

    new_pool = jnp.concatenate([pool_hist.astype(u.dtype), u], axis=1)[:, -POOL_HIST:]
    return y.astype(x.dtype), k_rows, v_rows, new_pool


def setup_inputs(seed: int = 0) -> dict:
    key = jax.random.key(seed)
    ks = jax.random.split(key, 20)
    f32 = jnp.float32
    nrm = lambda k, s: jax.random.normal(k, s, f32)
    return {
        "x_prompt": nrm(ks[0], (BATCH, SEQ, D_MODEL)),
        "x_sample": nrm(ks[1], (DEC_BATCH, DEC_SEQ, D_MODEL)),
        "cache_k": nrm(ks[2], (DEPTH, DEC_BATCH, PAST_LEN, N_HEADS, 2 * HEAD_DIM)),
        "cache_v": nrm(ks[3], (DEPTH, DEC_BATCH, PAST_LEN, N_HEADS, V_DIM)),
        "state_pool": nrm(ks[4], (DEPTH, DEC_BATCH, POOL_HIST, POOL_WIDTH)),
        "w_in": nrm(ks[5], (DEPTH, D_MODEL, IN_WIDTH)) * D_MODEL ** -0.5,
        "pool_w": nrm(ks[6], (DEPTH, POOL_GROUPS, POOL_GROUP_DIM, POOL_GROUP_DIM)) * POOL_GROUP_DIM ** -0.5,
        "pool_b": 0.01 * nrm(ks[7], (DEPTH, POOL_GROUPS, POOL_GROUP_DIM)),
        "pool_scale": 1.0 + 0.1 * nrm(ks[8], (DEPTH, POOL_WIDTH)),
        "lambda_q1": 0.1 * nrm(ks[9], (DEPTH, HEAD_DIM)),
        "lambda_k1": 0.1 * nrm(ks[10], (DEPTH, HEAD_DIM)),
        "lambda_q2": 0.1 * nrm(ks[11], (DEPTH, HEAD_DIM)),
        "lambda_k2": 0.1 * nrm(ks[12], (DEPTH, HEAD_DIM)),
        "subln_g": 1.0 + 0.01 * nrm(ks[13], (DEPTH, V_DIM)),
        "w_out": nrm(ks[14], (DEPTH, MIX_WIDTH, D_MODEL)) * (MIX_WIDTH ** -0.5) * DEEPNORM_BETA,
        "ln_g": 1.0 + 0.01 * nrm(ks[15], (DEPTH, D_MODEL)),
        "ln_b": 0.01 * nrm(ks[16], (DEPTH, D_MODEL)),
    }


def reference(x_prompt, x_sample, cache_k, cache_v, state_pool,
              w_in, pool_w, pool_b, pool_scale, lambda_q1, lambda_k1, lambda_q2, lambda_k2,
              subln_g, w_out, ln_g, ln_b):
    yp, ys = x_prompt, x_sample
    kp_l, vp_l, pp_l, ks_l, vs_l, ps_l = [], [], [], [], [], []
    n_valid_sample = min(PAST_LEN, POOL_HIST)
    for l in range(DEPTH):
        lam_init = _lambda_init(l)
        params = (w_in[l], pool_w[l], pool_b[l], pool_scale[l], lambda_q1[l], lambda_k1[l],
                  lambda_q2[l], lambda_k2[l], subln_g[l], w_out[l], ln_g[l], ln_b[l])
        zero_hist = jnp.zeros((yp.shape[0], POOL_HIST, POOL_WIDTH), yp.dtype)
        yp, kp, vp, pp = _layer(yp, zero_hist, 0, None, None, True, lam_init, *params)
        ys, kn, vn, pn = _layer(ys, state_pool[l], n_valid_sample, cache_k[l], cache_v[l],
                                False, lam_init, *params)
        kp_l.append(kp); vp_l.append(vp); pp_l.append(pp)
        ks_l.append(kn); vs_l.append(vn); ps_l.append(pn)
    k_prompt = jnp.stack(kp_l)
    v_prompt = jnp.stack(vp_l)
    pool_prompt = jnp.stack(pp_l)
    k_sample = jnp.stack(ks_l)
    v_sample = jnp.stack(vs_l)
    pool_sample = jnp.stack(ps_l)
    return (yp, ys, k_prompt, v_prompt, pool_prompt, k_sample, v_sample, pool_sample)
```

```python
import functools
import math

import numpy as np
import jax
import jax.numpy as jnp
from jax import lax
from jax.experimental import pallas as pl
from jax.experimental.pallas import tpu as pltpu

CHUNK = 64
POOL_WINDOWS = (2, 4, 8, 16)
POOL_GROUPS = len(POOL_WINDOWS)
HEAD_DIM = 64
V_DIM = 2 * HEAD_DIM
ATTN_SCALE = HEAD_DIM ** -0.5
LN_EPS = 1e-5
SUBLN_EPS = 1e-5

HIST_ROWS = 16
V7X_VMEM_LIMIT_BYTES = 56 * 1024 * 1024

_BF16 = jnp.bfloat16
_F32 = jnp.float32


def _lambda_init(layer_idx):
    return 0.8 - 0.6 * math.exp(-0.3 * layer_idx)


def _sigmoid(x):
    return 1.0 / (1.0 + jnp.exp(-x))


def _dot(a, b):
    return jnp.dot(a, b, preferred_element_type=_F32)


def _dot_nt(a, b):
    return lax.dot_general(a, b, (((1,), (1,)), ((), ())), preferred_element_type=_F32)


def _proj_kernel(x_ref, hist_ref, w_ref, pw_ref, pb_ref, ps_ref,
                 krow_ref, vrow_ref, q_ref, k_ref, v_ref, sg_ref, mp_ref, pst_ref,
                 carry_ref, *, nb, rows, n_valid, pool_width, attn_width, n_heads):
    i = pl.program_id(1)
    pw, aw = pool_width, attn_width
    group_dim = pw // POOL_GROUPS

    @pl.when(i == 0)
    def _():
        carry_ref[...] = hist_ref[...]

    xb = x_ref[...].astype(_BF16)
    u = _dot(xb, w_ref[:, 0:pw])
    g_pool = _dot(xb, w_ref[:, pw:2 * pw])

    ext_rows = HIST_ROWS + rows
    lane = lax.broadcasted_iota(jnp.int32, (ext_rows, pw), 1)
    row = lax.broadcasted_iota(jnp.int32, (ext_rows, pw), 0)
    t_pos = i * rows + row - HIST_ROWS
    win = jnp.full((ext_rows, pw), POOL_WINDOWS[-1], jnp.int32)
    for g in range(POOL_GROUPS - 2, -1, -1):
        win = jnp.where(lane < (g + 1) * group_dim, POOL_WINDOWS[g], win)
    cnt = jnp.maximum(jnp.minimum(win, n_valid + t_pos + 1), 1).astype(_F32)
    pooled_parts = []
    for b in range(nb):
        u_b = u[b * rows:(b + 1) * rows]
        ext = jnp.concatenate([carry_ref[b], u_b], axis=0)
        sums = {1: ext}
        w = 1
        while w < POOL_WINDOWS[-1]:
            sums[2 * w] = sums[w] + pltpu.roll(sums[w], shift=w, axis=0)
            w *= 2
        sel = sums[POOL_WINDOWS[-1]]
        for g in range(POOL_GROUPS - 2, -1, -1):
            sel = jnp.where(lane < (g + 1) * group_dim, sums[POOL_WINDOWS[g]], sel)
        pooled_parts.append((sel / cnt - ext)[HIST_ROWS:])
        carry_ref[b] = u_b[rows - HIST_ROWS:]
        pst_ref[b] = u_b[rows - HIST_ROWS:]
    pooled = pooled_parts[0] if nb == 1 else jnp.concatenate(pooled_parts, axis=0)
    mixed = (_dot(pooled.astype(_BF16), pw_ref[...]) + pb_ref[...]) * ps_ref[...]
    mp_ref[...] = (g_pool * _sigmoid(g_pool) * mixed).astype(_BF16)

    c0 = 2 * pw
    q = _dot(xb, w_ref[:, c0:c0 + aw]) * ATTN_SCALE
    k = _dot(xb, w_ref[:, c0 + aw:c0 + 2 * aw])
    v = _dot(xb, w_ref[:, c0 + 2 * aw:c0 + 3 * aw])
    g_attn = _dot(xb, w_ref[:, c0 + 3 * aw:c0 + 4 * aw])
    krow_ref[...] = k
    vrow_ref[...] = v
    sg_ref[...] = (g_attn * _sigmoid(g_attn)).astype(_BF16)
    qb, kb, vb = q.astype(_BF16), k.astype(_BF16), v.astype(_BF16)
    for b in range(nb):
        for h in range(n_heads):
            rs, cs = slice(b * rows, (b + 1) * rows), slice(h * V_DIM, (h + 1) * V_DIM)
            q_ref[b, h] = qb[rs, cs]
            k_ref[b, h] = kb[rs, cs]
            v_ref[b, h] = vb[rs, cs]


def _in_proj(x2d, hist, w_in_b, pw_bd, pool_b, pool_scale, *, n_seq, seq_len, nb, rows, n_valid):
    d_model = x2d.shape[1]
    pw = pw_bd.shape[0]
    aw = (w_in_b.shape[1] - 2 * pw) // 4
    n_heads = aw // V_DIM
    assert n_seq % nb == 0 and seq_len % rows == 0 and rows >= HIST_ROWS and rows % 8 == 0
    assert nb == 1 or rows == seq_len
    n_tiles = seq_len // rows
    tm = nb * rows
    grid = (n_seq // nb, n_tiles)
    row_map = lambda s, i: (s * n_tiles + i, 0)
    const2 = lambda s, i: (0, 0)
    hm_shape = jax.ShapeDtypeStruct((n_seq, n_heads, seq_len, V_DIM), _BF16)
    hm_spec = pl.BlockSpec((nb, n_heads, rows, V_DIM), lambda s, i: (s, 0, i, 0))
    n_rows = n_seq * seq_len
    kernel = functools.partial(_proj_kernel, nb=nb, rows=rows, n_valid=n_valid, pool_width=pw,
                               attn_width=aw, n_heads=n_heads)
    return pl.pallas_call(
        kernel,
        grid=grid,
        in_specs=[
            pl.BlockSpec((tm, d_model), row_map),
            pl.BlockSpec((nb, HIST_ROWS, pw), lambda s, i: (s, 0, 0)),
            pl.BlockSpec(w_in_b.shape, const2),
            pl.BlockSpec(pw_bd.shape, const2),
            pl.BlockSpec((1, pw), const2),
            pl.BlockSpec((1, pw), const2),
        ],
        out_specs=[
            pl.BlockSpec((tm, aw), row_map),
            pl.BlockSpec((tm, aw), row_map),
            hm_spec, hm_spec, hm_spec,
            pl.BlockSpec((tm, aw), row_map),
            pl.BlockSpec((tm, pw), row_map),
            pl.BlockSpec((nb, HIST_ROWS, pw), lambda s, i: (s, 0, 0)),
        ],
        out_shape=[
            jax.ShapeDtypeStruct((n_rows, aw), _F32),
            jax.ShapeDtypeStruct((n_rows, aw), _F32),
            hm_shape, hm_shape, hm_shape,
            jax.ShapeDtypeStruct((n_rows, aw), _BF16),
            jax.ShapeDtypeStruct((n_rows, pw), _BF16),
            jax.ShapeDtypeStruct((n_seq, HIST_ROWS, pw), _F32),
        ],
        scratch_shapes=[pltpu.VMEM((nb, HIST_ROWS, pw), _F32)],
        compiler_params=pltpu.CompilerParams(
            dimension_semantics=("arbitrary", "arbitrary"),
            vmem_limit_bytes=V7X_VMEM_LIMIT_BYTES),
        name="in_proj_pool",
    )(x2d, hist, w_in_b, pw_bd, pool_b, pool_scale)


def _lambda_full(lam_ref, lam_init):
    lp = lam_ref[...]
    a1 = jnp.sum(lp[0:1] * lp[1:2], axis=1, keepdims=True)
    a2 = jnp.sum(lp[2:3] * lp[3:4], axis=1, keepdims=True)
    return jnp.exp(a1) - jnp.exp(a2) + lam_init


def _split_maps(q):
    lane = lax.broadcasted_iota(jnp.int32, q.shape, 1)
    zero = jnp.zeros_like(q)
    return jnp.where(lane < HEAD_DIM, q, zero), jnp.where(lane >= HEAD_DIM, q, zero)


def _finish_heads(o1, l1, o2, l2, lam, lam_init, g, sg):
    o = o1 / l1 - lam * (o2 / l2)
    o = o * lax.rsqrt(jnp.mean(o * o, axis=-1, keepdims=True) + SUBLN_EPS)
    o = o * g * (1.0 - lam_init)
    return (sg.astype(_F32) * o).astype(_BF16)


def _flash_kernel(q_ref, k_ref, v_ref, sg_ref, lam_ref, g_ref, o_ref,
                  m_ref, l_ref, acc_ref, *, tq, lam_init):
    qi = pl.program_id(2)
    qs = _split_maps(q_ref[0, 0])
    m_ref[...] = jnp.full(m_ref.shape, -jnp.inf, _F32)
    l_ref[...] = jnp.zeros(l_ref.shape, _F32)
    acc_ref[...] = jnp.zeros(acc_ref.shape, _F32)

    def step(j, masked):
        start = pl.multiple_of(j * tq, tq)
        k = k_ref[0, 0, pl.ds(start, tq), :]
        v = v_ref[0, 0, pl.ds(start, tq), :]
        if masked:
            qc = lax.broadcasted_iota(jnp.int32, (tq, tq), 0) // CHUNK
            kc = lax.broadcasted_iota(jnp.int32, (tq, tq), 1) // CHUNK
            visible = kc <= qc
        for c in range(2):
            s = _dot_nt(qs[c], k)
            if masked:
                s = jnp.where(visible, s, -jnp.inf)
            m_prev = m_ref[c]
            m_new = jnp.maximum(m_prev, jnp.max(s, axis=1, keepdims=True))
            alpha = jnp.exp(m_prev - m_new)
            p = jnp.exp(s - m_new)
            l_ref[c] = alpha * l_ref[c] + jnp.sum(p, axis=1, keepdims=True)
            acc_ref[c] = alpha * acc_ref[c] + _dot(p.astype(_BF16), v)
            m_ref[c] = m_new

    lax.fori_loop(0, qi, lambda j, carry: (step(j, False), carry)[1], 0)
    step(qi, True)
    lam = _lambda_full(lam_ref, lam_init)
    o_ref[0] = _finish_heads(acc_ref[0], l_ref[0], acc_ref[1], l_ref[1], lam, lam_init,
                             g_ref[...], sg_ref[0])


def _flash_attention(q_hm, k_hm, v_hm, sg, lam_p, subln_g, *, tq, lam_init):
    n_seq, n_heads, seq_len, _ = q_hm.shape
    assert seq_len % tq == 0 and tq % CHUNK == 0
    kv_spec = pl.BlockSpec((1, 1, seq_len, V_DIM), lambda b, h, i: (b, h, 0, 0))
    row_spec = pl.BlockSpec((1, tq, V_DIM), lambda b, h, i: (b, i, h))
    return pl.pallas_call(
        functools.partial(_flash_kernel, tq=tq, lam_init=lam_init),
        grid=(n_seq, n_heads, seq_len // tq),
        in_specs=[
            pl.BlockSpec((1, 1, tq, V_DIM), lambda b, h, i: (b, h, i, 0)),
            kv_spec, kv_spec, row_spec,
            pl.BlockSpec(lam_p.shape, lambda b, h, i: (0, 0)),
            pl.BlockSpec(subln_g.shape, lambda b, h, i: (0, 0)),
        ],
        out_specs=row_spec,
        out_shape=jax.ShapeDtypeStruct(sg.shape, _BF16),
        scratch_shapes=[pltpu.VMEM((2, tq, 1), _F32), pltpu.VMEM((2, tq, 1), _F32),
                        pltpu.VMEM((2, tq, V_DIM), _F32)],
        compiler_params=pltpu.CompilerParams(
            dimension_semantics=("arbitrary", "arbitrary", "arbitrary"),
            vmem_limit_bytes=V7X_VMEM_LIMIT_BYTES),
        name="flash_diff_attn",
    )(q_hm, k_hm, v_hm, sg, lam_p, subln_g)


def _decode_attn_kernel(q_ref, k_ref, v_ref, ck_ref, cv_ref, sg_ref, lam_ref, g_ref, o_ref,
                        *, n_heads, vis_past, vis_new, lam_init):
    lam = _lambda_full(lam_ref, lam_init)
    for h in range(n_heads):
        cs = slice(h * V_DIM, (h + 1) * V_DIM)
        qs = _split_maps(q_ref[0, h])
        k_new, v_new = k_ref[0, h], v_ref[0, h]
        k_past = ck_ref[0, :, cs].astype(_BF16)
        v_past = cv_ref[0, :, cs].astype(_BF16)
        outs = []
        for c in range(2):
            s_past = _dot_nt(qs[c], k_past)
            s_new = _dot_nt(qs[c], k_new)
            if vis_past is not None:
                s_past = jnp.where(vis_past, s_past, -jnp.inf)
            if vis_new is not None:
                s_new = jnp.where(vis_new, s_new, -jnp.inf)
            m = jnp.maximum(jnp.max(s_past, axis=1, keepdims=True),
                            jnp.max(s_new, axis=1, keepdims=True))
            p_past = jnp.exp(s_past - m)
            p_new = jnp.exp(s_new - m)
            l = jnp.sum(p_past, axis=1, keepdims=True) + jnp.sum(p_new, axis=1, keepdims=True)
            o = _dot(p_past.astype(_BF16), v_past) + _dot(p_new.astype(_BF16), v_new)
            outs += [o, l]
        o_ref[0, :, cs] = _finish_heads(*outs, lam, lam_init, g_ref[...], sg_ref[0, :, cs])


def _static_visibility(q_pos, k_pos):
    vis = (k_pos[None, :] // CHUNK) <= (q_pos[:, None] // CHUNK)
    return None if vis.all() else jnp.asarray(vis)


def _decode_attention(q_hm, k_hm, v_hm, ck, cv, sg, lam_p, subln_g, *, lam_init):
    n_seq, n_heads, t_new, _ = q_hm.shape
    past = ck.shape[1]
    q_pos = past + np.arange(t_new)
    vis_past = _static_visibility(q_pos, np.arange(past))
    vis_new = _static_visibility(q_pos, q_pos)
    new_spec = pl.BlockSpec((1, n_heads, t_new, V_DIM), lambda b: (b, 0, 0, 0))
    cache_spec = pl.BlockSpec((1, past, n_heads * V_DIM), lambda b: (b, 0, 0))
    row_spec = pl.BlockSpec((1, t_new, n_heads * V_DIM), lambda b: (b, 0, 0))
    return pl.pallas_call(
        functools.partial(_decode_attn_kernel, n_heads=n_heads, vis_past=vis_past, vis_new=vis_new,
                          lam_init=lam_init),
        grid=(n_seq,),
        in_specs=[new_spec, new_spec, new_spec, cache_spec, cache_spec, row_spec,
                  pl.BlockSpec(lam_p.shape, lambda b: (0, 0)),
                  pl.BlockSpec(subln_g.shape, lambda b: (0, 0))],
        out_specs=row_spec,
        out_shape=jax.ShapeDtypeStruct(sg.shape, _BF16),
        compiler_params=pltpu.CompilerParams(
            dimension_semantics=("arbitrary",), vmem_limit_bytes=V7X_VMEM_LIMIT_BYTES),
        name="decode_diff_attn",
    )(q_hm, k_hm, v_hm, ck, cv, sg, lam_p, subln_g)


def _out_kernel(x_ref, mp_ref, ma_ref, w_ref, g_ref, b_ref, y_ref, *, pool_width, alpha):
    out = _dot(mp_ref[...], w_ref[0:pool_width, :]) + _dot(ma_ref[...], w_ref[pool_width:, :])
    z = alpha * x_ref[...] + out
    mu = jnp.mean(z, axis=-1, keepdims=True)
    zc = z - mu
    var = jnp.mean(zc * zc, axis=-1, keepdims=True)
    y_ref[...] = zc * lax.rsqrt(var + LN_EPS) * g_ref[...] + b_ref[...]


def _out_proj(x2d, mp, ma, w_out_b, ln_g, ln_b, *, tm, alpha):
    n_rows, d_model = x2d.shape
    pw, aw = mp.shape[1], ma.shape[1]
    assert n_rows % tm == 0
    row = lambda i: (i, 0)
    const = lambda i: (0, 0)
    return pl.pallas_call(
        functools.partial(_out_kernel, pool_width=pw, alpha=alpha),
        grid=(n_rows // tm,),
        in_specs=[pl.BlockSpec((tm, d_model), row), pl.BlockSpec((tm, pw), row),
                  pl.BlockSpec((tm, aw), row), pl.BlockSpec(w_out_b.shape, const),
                  pl.BlockSpec((1, d_model), const), pl.BlockSpec((1, d_model), const)],
        out_specs=pl.BlockSpec((tm, d_model), row),
        out_shape=jax.ShapeDtypeStruct((n_rows, d_model), _F32),
        compiler_params=pltpu.CompilerParams(
            dimension_semantics=("arbitrary",), vmem_limit_bytes=V7X_VMEM_LIMIT_BYTES),
        name="out_proj_ln",
    )(x2d, mp, ma, w_out_b, ln_g, ln_b)


def _block_diag(pool_w):
    g, c, d = pool_w.shape
    out = jnp.zeros((g * c, g * d), pool_w.dtype)
    for i in range(g):
        out = out.at[i * c:(i + 1) * c, i * d:(i + 1) * d].set(pool_w[i])
    return out


def _layer(x, hist, n_valid, k_past, v_past, lam_init, alpha, params, *, rows, tq, tm_out):
    (w_in, pool_w, pool_b, pool_scale, lq1, lk1, lq2, lk2, subln_g, w_out, ln_g, ln_b) = params
    n_seq, seq_len, d_model = x.shape
    pw = pool_scale.shape[0]
    aw = w_out.shape[0] - pw
    n_heads = aw // V_DIM
    x2d = x.reshape(n_seq * seq_len, d_model)
    w_in_b, w_out_b = w_in.astype(_BF16), w_out.astype(_BF16)
    pw_bd = _block_diag(pool_w).astype(_BF16)
    lam_p = jnp.stack([lq1, lk1, lq2, lk2]).astype(_F32)
    g2d = subln_g.reshape(1, V_DIM).astype(_F32)
    nb = n_seq if k_past is not None else 1
    krow, vrow, q_hm, k_hm, v_hm, sg, mp, pst = _in_proj(
        x2d, hist, w_in_b, pw_bd, pool_b.reshape(1, pw).astype(_F32),
        pool_scale.reshape(1, pw).astype(_F32),
        n_seq=n_seq, seq_len=seq_len, nb=nb, rows=rows, n_valid=n_valid)
    sg3 = sg.reshape(n_seq, seq_len, aw)
    if k_past is None:
        ma = _flash_attention(q_hm, k_hm, v_hm, sg3, lam_p, g2d, tq=tq, lam_init=lam_init)
    else:
        past = k_past.shape[1]
        ma = _decode_attention(q_hm, k_hm, v_hm, k_past.reshape(n_seq, past, aw),
                               v_past.reshape(n_seq, past, aw), sg3, lam_p, g2d, lam_init=lam_init)
    y = _out_proj(x2d, mp, ma.reshape(n_seq * seq_len, aw), w_out_b,
                  ln_g.reshape(1, d_model).astype(_F32), ln_b.reshape(1, d_model).astype(_F32),
                  tm=tm_out, alpha=alpha)
    k_rows = krow.reshape(n_seq, seq_len, n_heads, V_DIM)
    v_rows = vrow.reshape(n_seq, seq_len, n_heads, V_DIM)
    return y.reshape(n_seq, seq_len, d_model), k_rows, v_rows, pst


def kernel(x_prompt, x_sample, cache_k, cache_v, state_pool, w_in, pool_w, pool_b, pool_scale,
           lambda_q1, lambda_k1, lambda_q2, lambda_k2, subln_g, w_out, ln_g, ln_b):
    depth = w_in.shape[0]
    pw = pool_scale.shape[1]
    pool_hist = max(POOL_WINDOWS) - 1
    alpha = (2.0 * depth) ** 0.25
    n_valid_sample = min(cache_k.shape[2], pool_hist)
    yp, ys = x_prompt, x_sample
    outs = [[] for _ in range(6)]
    for l in range(depth):
        lam_init = _lambda_init(l)
        params = (w_in[l], pool_w[l], pool_b[l], pool_scale[l], lambda_q1[l], lambda_k1[l],
                  lambda_q2[l], lambda_k2[l], subln_g[l], w_out[l], ln_g[l], ln_b[l])
        zero_hist = jnp.zeros((yp.shape[0], HIST_ROWS, pw), _F32)
        yp, kp, vp, pp = _layer(yp, zero_hist, 0, None, None, lam_init, alpha, params,
                                rows=512, tq=512, tm_out=512)
        hist_s = jnp.pad(state_pool[l].astype(_F32), ((0, 0), (HIST_ROWS - pool_hist, 0), (0, 0)))
        ys, kn, vn, pn = _layer(ys, hist_s, n_valid_sample, cache_k[l], cache_v[l], lam_init, alpha,
                                params, rows=ys.shape[1], tq=None,
                                tm_out=ys.shape[0] * ys.shape[1])
        for lst, val in zip(outs, (kp, vp, pp[:, HIST_ROWS - pool_hist:], kn, vn,
                                   pn[:, HIST_ROWS - pool_hist:])):
            lst.append(val)
    return (yp, ys) + tuple(jnp.stack(o) for o in outs)
```

```python
import functools
import math

import numpy as np
import jax
import jax.numpy as jnp
from jax import lax
from jax.experimental import pallas as pl
from jax.experimental.pallas import tpu as pltpu

CHUNK = 64
POOL_WINDOWS = (2, 4, 8, 16)
POOL_GROUPS = len(POOL_WINDOWS)
HEAD_DIM = 64
V_DIM = 2 * HEAD_DIM
ATTN_SCALE = HEAD_DIM ** -0.5
Q_SCALE = ATTN_SCALE * math.log2(math.e)
LN_EPS = 1e-5
SUBLN_EPS = 1e-5

HIST_ROWS = 16
V7X_VMEM_LIMIT_BYTES = 56 * 1024 * 1024

_BF16 = jnp.bfloat16
_F32 = jnp.float32


def _lambda_init(layer_idx):
    return 0.8 - 0.6 * math.exp(-0.3 * layer_idx)


def _sigmoid(x):
    return 1.0 / (1.0 + jnp.exp(-x))


def _dot(a, b):
    return jnp.dot(a, b, preferred_element_type=_F32)


def _dot_nt(a, b):
    return lax.dot_general(a, b, (((1,), (1,)), ((), ())), preferred_element_type=_F32)


def _proj_kernel(x_ref, hist_ref, w_ref, pw_ref, pb_ref, ps_ref,
                 krow_ref, vrow_ref, q_ref, k_ref, v_ref, sg_ref, mp_ref, pst_ref,
                 carry_ref, *, nb, rows, n_valid, pool_width, attn_width, n_heads, transposed):
    i = pl.program_id(1)
    pw, aw = pool_width, attn_width
    group_dim = pw // POOL_GROUPS

    @pl.when(i == 0)
    def _():
        carry_ref[...] = hist_ref[...]

    xb = x_ref[...].astype(_BF16)
    u = _dot(xb, w_ref[:, 0:pw])
    g_pool = _dot(xb, w_ref[:, pw:2 * pw])

    ext_rows = HIST_ROWS + rows
    lane = lax.broadcasted_iota(jnp.int32, (ext_rows, pw), 1)
    row = lax.broadcasted_iota(jnp.int32, (ext_rows, pw), 0)
    t_pos = i * rows + row - HIST_ROWS
    win = jnp.full((ext_rows, pw), POOL_WINDOWS[-1], jnp.int32)
    for g in range(POOL_GROUPS - 2, -1, -1):
        win = jnp.where(lane < (g + 1) * group_dim, POOL_WINDOWS[g], win)
    cnt = jnp.maximum(jnp.minimum(win, n_valid + t_pos + 1), 1).astype(_F32)
    pooled_parts = []
    for b in range(nb):
        u_b = u[b * rows:(b + 1) * rows]
        ext = jnp.concatenate([carry_ref[b], u_b], axis=0)
        sums = {1: ext}
        w = 1
        while w < POOL_WINDOWS[-1]:
            sums[2 * w] = sums[w] + pltpu.roll(sums[w], shift=w, axis=0)
            w *= 2
        sel = sums[POOL_WINDOWS[-1]]
        for g in range(POOL_GROUPS - 2, -1, -1):
            sel = jnp.where(lane < (g + 1) * group_dim, sums[POOL_WINDOWS[g]], sel)
        pooled_parts.append((sel / cnt - ext)[HIST_ROWS:])
        carry_ref[b] = u_b[rows - HIST_ROWS:]
        pst_ref[b] = u_b[rows - HIST_ROWS:]
    pooled = pooled_parts[0] if nb == 1 else jnp.concatenate(pooled_parts, axis=0)
    mixed = (_dot(pooled.astype(_BF16), pw_ref[...]) + pb_ref[...]) * ps_ref[...]
    mp_ref[...] = (g_pool * _sigmoid(g_pool) * mixed).astype(_BF16)

    c0 = 2 * pw
    q = _dot(xb, w_ref[:, c0:c0 + aw]) * Q_SCALE
    k = _dot(xb, w_ref[:, c0 + aw:c0 + 2 * aw])
    v = _dot(xb, w_ref[:, c0 + 2 * aw:c0 + 3 * aw])
    g_attn = _dot(xb, w_ref[:, c0 + 3 * aw:c0 + 4 * aw])
    krow_ref[...] = k
    vrow_ref[...] = v
    sg_ref[...] = (g_attn * _sigmoid(g_attn)).astype(_BF16)
    for b in range(nb):
        for h in range(n_heads):
            rs, cs = slice(b * rows, (b + 1) * rows), slice(h * V_DIM, (h + 1) * V_DIM)
            k_ref[b, h] = k[rs, cs].astype(_BF16)
            if transposed:
                q_ref[b, h] = q[rs, cs].T.astype(_BF16)
                v_ref[b, h] = v[rs, cs].T.astype(_BF16)
            else:
                q_ref[b, h] = q[rs, cs].astype(_BF16)
                v_ref[b, h] = v[rs, cs].astype(_BF16)


def _in_proj(x2d, hist, w_in_b, pw_bd, pool_b, pool_scale, *, n_seq, seq_len, nb, rows, n_valid,
             transposed):
    d_model = x2d.shape[1]
    pw = pw_bd.shape[0]
    aw = (w_in_b.shape[1] - 2 * pw) // 4
    n_heads = aw // V_DIM
    assert n_seq % nb == 0 and seq_len % rows == 0 and rows >= HIST_ROWS and rows % 8 == 0
    assert nb == 1 or rows == seq_len
    n_tiles = seq_len // rows
    tm = nb * rows
    grid = (n_seq // nb, n_tiles)
    row_map = lambda s, i: (s * n_tiles + i, 0)
    const2 = lambda s, i: (0, 0)
    hm_shape = jax.ShapeDtypeStruct((n_seq, n_heads, seq_len, V_DIM), _BF16)
    hm_spec = pl.BlockSpec((nb, n_heads, rows, V_DIM), lambda s, i: (s, 0, i, 0))
    if transposed:
        qv_shape = jax.ShapeDtypeStruct((n_seq, n_heads, V_DIM, seq_len), _BF16)
        qv_spec = pl.BlockSpec((nb, n_heads, V_DIM, rows), lambda s, i: (s, 0, 0, i))
    else:
        qv_shape, qv_spec = hm_shape, hm_spec
    n_rows = n_seq * seq_len
    kernel = functools.partial(_proj_kernel, nb=nb, rows=rows, n_valid=n_valid, pool_width=pw,
                               attn_width=aw, n_heads=n_heads, transposed=transposed)
    return pl.pallas_call(
        kernel,
        grid=grid,
        in_specs=[
            pl.BlockSpec((tm, d_model), row_map),
            pl.BlockSpec((nb, HIST_ROWS, pw), lambda s, i: (s, 0, 0)),
            pl.BlockSpec(w_in_b.shape, const2),
            pl.BlockSpec(pw_bd.shape, const2),
            pl.BlockSpec((1, pw), const2),
            pl.BlockSpec((1, pw), const2),
        ],
        out_specs=[
            pl.BlockSpec((tm, aw), row_map),
            pl.BlockSpec((tm, aw), row_map),
            qv_spec, hm_spec, qv_spec,
            pl.BlockSpec((tm, aw), row_map),
            pl.BlockSpec((tm, pw), row_map),
            pl.BlockSpec((nb, HIST_ROWS, pw), lambda s, i: (s, 0, 0)),
        ],
        out_shape=[
            jax.ShapeDtypeStruct((n_rows, aw), _F32),
            jax.ShapeDtypeStruct((n_rows, aw), _F32),
            qv_shape, hm_shape, qv_shape,
            jax.ShapeDtypeStruct((n_rows, aw), _BF16),
            jax.ShapeDtypeStruct((n_rows, pw), _BF16),
            jax.ShapeDtypeStruct((n_seq, HIST_ROWS, pw), _F32),
        ],
        scratch_shapes=[pltpu.VMEM((nb, HIST_ROWS, pw), _F32)],
        compiler_params=pltpu.CompilerParams(
            dimension_semantics=("arbitrary", "arbitrary"),
            vmem_limit_bytes=V7X_VMEM_LIMIT_BYTES),
        name="in_proj_pool",
    )(x2d, hist, w_in_b, pw_bd, pool_b, pool_scale)


def _lambda_full(lam_ref, lam_init):
    lp = lam_ref[...]
    a1 = jnp.sum(lp[0:1] * lp[1:2], axis=1, keepdims=True)
    a2 = jnp.sum(lp[2:3] * lp[3:4], axis=1, keepdims=True)
    return jnp.exp(a1) - jnp.exp(a2) + lam_init


def _split_maps(q):
    lane = lax.broadcasted_iota(jnp.int32, q.shape, 1)
    zero = jnp.zeros_like(q)
    return jnp.where(lane < HEAD_DIM, q, zero), jnp.where(lane >= HEAD_DIM, q, zero)


def _finish_heads(o1, l1, o2, l2, lam, lam_init, g, sg):
    o = o1 / l1 - lam * (o2 / l2)
    o = o * lax.rsqrt(jnp.mean(o * o, axis=-1, keepdims=True) + SUBLN_EPS)
    o = o * g * (1.0 - lam_init)
    return (sg.astype(_F32) * o).astype(_BF16)


def _sublane_all(op, x):
    for shift in (4, 2, 1):
        x = op(x, pltpu.roll(x, shift=shift, axis=0))
    return x


def _flash_kernel(qT_ref, k_ref, vT_ref, sg_ref, lam_ref, g_ref, o_ref,
                  m_ref, l_ref, acc_ref, *, tq, tk, lam_init):
    qi = pl.program_id(2)
    qT = qT_ref[0, 0]
    row = lax.broadcasted_iota(jnp.int32, qT.shape, 0)
    zero = jnp.zeros_like(qT)
    qz = (jnp.where(row < HEAD_DIM, qT, zero), jnp.where(row >= HEAD_DIM, qT, zero))
    m_ref[...] = jnp.full(m_ref.shape, -jnp.inf, _F32)
    l_ref[...] = jnp.zeros(l_ref.shape, _F32)
    acc_ref[...] = jnp.zeros(acc_ref.shape, _F32)
    n_full = (qi * tq) // tk

    def step(j, masked):
        start = pl.multiple_of(j * tk, tk)
        k = k_ref[0, 0, pl.ds(start, tk), :]
        vT = vT_ref[0, 0, :, pl.ds(start, tk)]
        if masked:
            k_pos = start + lax.broadcasted_iota(jnp.int32, (tk, tq), 0)
            q_pos = qi * tq + lax.broadcasted_iota(jnp.int32, (tk, tq), 1)
            visible = k_pos <= (q_pos | (CHUNK - 1))
        sTs = [_dot(k, qz[c]) for c in range(2)]
        for c in range(2):
            sT = sTs[c]
            if masked:
                sT = jnp.where(visible, sT, -jnp.inf)
            s3 = sT.reshape(tk // 8, 8, tq)
            m_prev = m_ref[c]
            m_new = jnp.maximum(m_prev, _sublane_all(jnp.maximum, jnp.max(s3, axis=0)))
            alpha = jnp.exp2(m_prev - m_new)
            p3 = jnp.exp2(s3 - m_new[None])
            l_ref[c] = alpha * l_ref[c] + jnp.sum(p3, axis=0)
            pv = _dot(vT, p3.reshape(tk, tq).astype(_BF16))
            acc_ref[c] = alpha[None] * acc_ref[c] + pv.reshape(V_DIM // 8, 8, tq)
            m_ref[c] = m_new

    lax.fori_loop(0, n_full, lambda j, carry: (step(j, False), carry)[1], 0)
    step(n_full, True)

    lam = _lambda_full(lam_ref, lam_init).reshape(1, 1, 1)
    l1 = _sublane_all(jnp.add, l_ref[0])
    l2 = _sublane_all(jnp.add, l_ref[1])
    oT = acc_ref[0] / l1[None] - lam * (acc_ref[1] / l2[None])
    ms = _sublane_all(jnp.add, jnp.sum(oT * oT, axis=0)) * (1.0 / V_DIM)
    oT = oT * lax.rsqrt(ms + SUBLN_EPS)[None]
    oT = oT.reshape(V_DIM, tq) * g_ref[...] * (1.0 - lam_init)
    o_ref[0] = (sg_ref[0].astype(_F32) * oT.T).astype(_BF16)


def _flash_attention(qT_hm, k_hm, vT_hm, sg, lam_p, subln_g_col, *, tq, tk, lam_init):
    n_seq, n_heads, seq_len, _ = k_hm.shape
    assert seq_len % tk == 0 and tk % tq == 0 and tq % CHUNK == 0 and tq % 128 == 0
    row_spec = pl.BlockSpec((1, tq, V_DIM), lambda b, h, i: (b, i, h))
    return pl.pallas_call(
        functools.partial(_flash_kernel, tq=tq, tk=tk, lam_init=lam_init),
        grid=(n_seq, n_heads, seq_len // tq),
        in_specs=[
            pl.BlockSpec((1, 1, V_DIM, tq), lambda b, h, i: (b, h, 0, i)),
            pl.BlockSpec((1, 1, seq_len, V_DIM), lambda b, h, i: (b, h, 0, 0)),
            pl.BlockSpec((1, 1, V_DIM, seq_len), lambda b, h, i: (b, h, 0, 0)),
            row_spec,
            pl.BlockSpec(lam_p.shape, lambda b, h, i: (0, 0)),
            pl.BlockSpec(subln_g_col.shape, lambda b, h, i: (0, 0)),
        ],
        out_specs=row_spec,
        out_shape=jax.ShapeDtypeStruct(sg.shape, _BF16),
        scratch_shapes=[pltpu.VMEM((2, 8, tq), _F32), pltpu.VMEM((2, 8, tq), _F32),
                        pltpu.VMEM((2, V_DIM // 8, 8, tq), _F32)],
        compiler_params=pltpu.CompilerParams(
            dimension_semantics=("arbitrary", "arbitrary", "arbitrary"),
            vmem_limit_bytes=V7X_VMEM_LIMIT_BYTES),
        name="flash_diff_attn",
    )(qT_hm, k_hm, vT_hm, sg, lam_p, subln_g_col)


def _decode_attn_kernel(q_ref, k_ref, v_ref, ck_ref, cv_ref, sg_ref, lam_ref, g_ref, o_ref,
                        *, n_heads, vis_past, vis_new, lam_init):
    lam = _lambda_full(lam_ref, lam_init)
    for h in range(n_heads):
        cs = slice(h * V_DIM, (h + 1) * V_DIM)
        qs = _split_maps(q_ref[0, h])
        k_new, v_new = k_ref[0, h], v_ref[0, h]
        k_past = ck_ref[0, :, cs].astype(_BF16)
        v_past = cv_ref[0, :, cs].astype(_BF16)
        outs = []
        for c in range(2):
            s_past = _dot_nt(qs[c], k_past)
            s_new = _dot_nt(qs[c], k_new)
            if vis_past is not None:
                s_past = jnp.where(vis_past, s_past, -jnp.inf)
            if vis_new is not None:
                s_new = jnp.where(vis_new, s_new, -jnp.inf)
            m = jnp.maximum(jnp.max(s_past, axis=1, keepdims=True),
                            jnp.max(s_new, axis=1, keepdims=True))
            p_past = jnp.exp2(s_past - m)
            p_new = jnp.exp2(s_new - m)
            l = jnp.sum(p_past, axis=1, keepdims=True) + jnp.sum(p_new, axis=1, keepdims=True)
            o = _dot(p_past.astype(_BF16), v_past) + _dot(p_new.astype(_BF16), v_new)
            outs += [o, l]
        o_ref[0, :, cs] = _finish_heads(*outs, lam, lam_init, g_ref[...], sg_ref[0, :, cs])


def _static_visibility(q_pos, k_pos):
    vis = (k_pos[None, :] // CHUNK) <= (q_pos[:, None] // CHUNK)
    return None if vis.all() else jnp.asarray(vis)


def _decode_attention(q_hm, k_hm, v_hm, ck, cv, sg, lam_p, subln_g, *, lam_init):
    n_seq, n_heads, t_new, _ = q_hm.shape
    past = ck.shape[1]
    q_pos = past + np.arange(t_new)
    vis_past = _static_visibility(q_pos, np.arange(past))
    vis_new = _static_visibility(q_pos, q_pos)
    new_spec = pl.BlockSpec((1, n_heads, t_new, V_DIM), lambda b: (b, 0, 0, 0))
    cache_spec = pl.BlockSpec((1, past, n_heads * V_DIM), lambda b: (b, 0, 0))
    row_spec = pl.BlockSpec((1, t_new, n_heads * V_DIM), lambda b: (b, 0, 0))
    return pl.pallas_call(
        functools.partial(_decode_attn_kernel, n_heads=n_heads, vis_past=vis_past, vis_new=vis_new,
                          lam_init=lam_init),
        grid=(n_seq,),
        in_specs=[new_spec, new_spec, new_spec, cache_spec, cache_spec, row_spec,
                  pl.BlockSpec(lam_p.shape, lambda b: (0, 0)),
                  pl.BlockSpec(subln_g.shape, lambda b: (0, 0))],
        out_specs=row_spec,
        out_shape=jax.ShapeDtypeStruct(sg.shape, _BF16),
        compiler_params=pltpu.CompilerParams(
            dimension_semantics=("arbitrary",), vmem_limit_bytes=V7X_VMEM_LIMIT_BYTES),
        name="decode_diff_attn",
    )(q_hm, k_hm, v_hm, ck, cv, sg, lam_p, subln_g)


def _out_kernel(x_ref, mp_ref, ma_ref, w_ref, g_ref, b_ref, y_ref, *, pool_width, alpha):
    out = _dot(mp_ref[...], w_ref[0:pool_width, :]) + _dot(ma_ref[...], w_ref[pool_width:, :])
    z = alpha * x_ref[...] + out
    mu = jnp.mean(z, axis=-1, keepdims=True)
    zc = z - mu
    var = jnp.mean(zc * zc, axis=-1, keepdims=True)
    y_ref[...] = zc * lax.rsqrt(var + LN_EPS) * g_ref[...] + b_ref[...]


def _out_proj(x2d, mp, ma, w_out_b, ln_g, ln_b, *, tm, alpha):
    n_rows, d_model = x2d.shape
    pw, aw = mp.shape[1], ma.shape[1]
    assert n_rows % tm == 0
    row = lambda i: (i, 0)
    const = lambda i: (0, 0)
    return pl.pallas_call(
        functools.partial(_out_kernel, pool_width=pw, alpha=alpha),
        grid=(n_rows // tm,),
        in_specs=[pl.BlockSpec((tm, d_model), row), pl.BlockSpec((tm, pw), row),
                  pl.BlockSpec((tm, aw), row), pl.BlockSpec(w_out_b.shape, const),
                  pl.BlockSpec((1, d_model), const), pl.BlockSpec((1, d_model), const)],
        out_specs=pl.BlockSpec((tm, d_model), row),
        out_shape=jax.ShapeDtypeStruct((n_rows, d_model), _F32),
        compiler_params=pltpu.CompilerParams(
            dimension_semantics=("arbitrary",), vmem_limit_bytes=V7X_VMEM_LIMIT_BYTES),
        name="out_proj_ln",
    )(x2d, mp, ma, w_out_b, ln_g, ln_b)


def _block_diag(pool_w):
    g, c, d = pool_w.shape
    out = jnp.zeros((g * c, g * d), pool_w.dtype)
    for i in range(g):
        out = out.at[i * c:(i + 1) * c, i * d:(i + 1) * d].set(pool_w[i])
    return out


def _layer(x, hist, n_valid, k_past, v_past, lam_init, alpha, params, *, rows, tq, tk, tm_out):
    (w_in, pool_w, pool_b, pool_scale, lq1, lk1, lq2, lk2, subln_g, w_out, ln_g, ln_b) = params
    n_seq, seq_len, d_model = x.shape
    pw = pool_scale.shape[0]
    aw = w_out.shape[0] - pw
    n_heads = aw // V_DIM
    x2d = x.reshape(n_seq * seq_len, d_model)
    w_in_b, w_out_b = w_in.astype(_BF16), w_out.astype(_BF16)
    pw_bd = _block_diag(pool_w).astype(_BF16)
    lam_p = jnp.stack([lq1, lk1, lq2, lk2]).astype(_F32)
    prompt = k_past is None
    nb = 1 if prompt else n_seq
    krow, vrow, q_hm, k_hm, v_hm, sg, mp, pst = _in_proj(
        x2d, hist, w_in_b, pw_bd, pool_b.reshape(1, pw).astype(_F32),
        pool_scale.reshape(1, pw).astype(_F32),
        n_seq=n_seq, seq_len=seq_len, nb=nb, rows=rows, n_valid=n_valid, transposed=prompt)
    sg3 = sg.reshape(n_seq, seq_len, aw)
    if prompt:
        ma = _flash_attention(q_hm, k_hm, v_hm, sg3, lam_p, subln_g.reshape(V_DIM, 1).astype(_F32),
                              tq=tq, tk=tk, lam_init=lam_init)
    else:
        past = k_past.shape[1]
        ma = _decode_attention(q_hm, k_hm, v_hm, k_past.reshape(n_seq, past, aw),
                               v_past.reshape(n_seq, past, aw), sg3, lam_p,
                               subln_g.reshape(1, V_DIM).astype(_F32), lam_init=lam_init)
    y = _out_proj(x2d, mp, ma.reshape(n_seq * seq_len, aw), w_out_b,
                  ln_g.reshape(1, d_model).astype(_F32), ln_b.reshape(1, d_model).astype(_F32),
                  tm=tm_out, alpha=alpha)
    k_rows = krow.reshape(n_seq, seq_len, n_heads, V_DIM)
    v_rows = vrow.reshape(n_seq, seq_len, n_heads, V_DIM)
    return y.reshape(n_seq, seq_len, d_model), k_rows, v_rows, pst


def kernel(x_prompt, x_sample, cache_k, cache_v, state_pool, w_in, pool_w, pool_b, pool_scale,
           lambda_q1, lambda_k1, lambda_q2, lambda_k2, subln_g, w_out, ln_g, ln_b):
    depth = w_in.shape[0]
    pw = pool_scale.shape[1]
    pool_hist = max(POOL_WINDOWS) - 1
    alpha = (2.0 * depth) ** 0.25
    n_valid_sample = min(cache_k.shape[2], pool_hist)
    yp, ys = x_prompt, x_sample
    outs = [[] for _ in range(6)]
    for l in range(depth):
        lam_init = _lambda_init(l)
        params = (w_in[l], pool_w[l], pool_b[l], pool_scale[l], lambda_q1[l], lambda_k1[l],
                  lambda_q2[l], lambda_k2[l], subln_g[l], w_out[l], ln_g[l], ln_b[l])
        zero_hist = jnp.zeros((yp.shape[0], HIST_ROWS, pw), _F32)
        yp, kp, vp, pp = _layer(yp, zero_hist, 0, None, None, lam_init, alpha, params,
                                rows=512, tq=512, tk=1024, tm_out=512)
        hist_s = jnp.pad(state_pool[l].astype(_F32), ((0, 0), (HIST_ROWS - pool_hist, 0), (0, 0)))
        ys, kn, vn, pn = _layer(ys, hist_s, n_valid_sample, cache_k[l], cache_v[l], lam_init, alpha,
                                params, rows=ys.shape[1], tq=None, tk=None,
                                tm_out=ys.shape[0] * ys.shape[1])
        for lst, val in zip(outs, (kp, vp, pp[:, HIST_ROWS - pool_hist:], kn, vn,
                                   pn[:, HIST_ROWS - pool_hist:])):
            lst.append(val)
    return (yp, ys) + tuple(jnp.stack(o) for o in outs)
```

```python
import functools
import math

import numpy as np
import jax
import jax.numpy as jnp
from jax import lax
from jax.experimental import pallas as pl
from jax.experimental.pallas import tpu as pltpu

CHUNK = 64
POOL_WINDOWS = (2, 4, 8, 16)
POOL_GROUPS = len(POOL_WINDOWS)
HEAD_DIM = 64
V_DIM = 2 * HEAD_DIM
ATTN_SCALE = HEAD_DIM ** -0.5
Q_SCALE = ATTN_SCALE * math.log2(math.e)
LN_EPS = 1e-5
SUBLN_EPS = 1e-5

BF16_ROW_TILE = 16
VT_ROWS = V_DIM + BF16_ROW_TILE

HIST_ROWS = 16
V7X_VMEM_LIMIT_BYTES = 56 * 1024 * 1024

_BF16 = jnp.bfloat16
_F32 = jnp.float32


def _lambda_init(layer_idx):
    return 0.8 - 0.6 * math.exp(-0.3 * layer_idx)


def _sigmoid(x):
    return 1.0 / (1.0 + jnp.exp(-x))


def _dot(a, b):
    return jnp.dot(a, b, preferred_element_type=_F32)


def _dot_nt(a, b):
    return lax.dot_general(a, b, (((1,), (1,)), ((), ())), preferred_element_type=_F32)


def _proj_kernel(x_ref, hist_ref, w_ref, pw_ref, pb_ref, ps_ref,
                 krow_ref, vrow_ref, q_ref, k_ref, v_ref, sg_ref, mp_ref, pst_ref,
                 carry_ref, *, nb, rows, n_valid, pool_width, attn_width, n_heads, transposed):
    i = pl.program_id(1)
    pw, aw = pool_width, attn_width
    group_dim = pw // POOL_GROUPS

    @pl.when(i == 0)
    def _():
        carry_ref[...] = hist_ref[...]

    xb = x_ref[...].astype(_BF16)
    u = _dot(xb, w_ref[:, 0:pw])
    g_pool = _dot(xb, w_ref[:, pw:2 * pw])

    ext_rows = HIST_ROWS + rows
    lane = lax.broadcasted_iota(jnp.int32, (ext_rows, pw), 1)
    row = lax.broadcasted_iota(jnp.int32, (ext_rows, pw), 0)
    t_pos = i * rows + row - HIST_ROWS
    win = jnp.full((ext_rows, pw), POOL_WINDOWS[-1], jnp.int32)
    for g in range(POOL_GROUPS - 2, -1, -1):
        win = jnp.where(lane < (g + 1) * group_dim, POOL_WINDOWS[g], win)
    cnt = jnp.maximum(jnp.minimum(win, n_valid + t_pos + 1), 1).astype(_F32)
    pooled_parts = []
    for b in range(nb):
        u_b = u[b * rows:(b + 1) * rows]
        ext = jnp.concatenate([carry_ref[b], u_b], axis=0)
        sums = {1: ext}
        w = 1
        while w < POOL_WINDOWS[-1]:
            sums[2 * w] = sums[w] + pltpu.roll(sums[w], shift=w, axis=0)
            w *= 2
        sel = sums[POOL_WINDOWS[-1]]
        for g in range(POOL_GROUPS - 2, -1, -1):
            sel = jnp.where(lane < (g + 1) * group_dim, sums[POOL_WINDOWS[g]], sel)
        pooled_parts.append((sel / cnt - ext)[HIST_ROWS:])
        carry_ref[b] = u_b[rows - HIST_ROWS:]
        pst_ref[b] = u_b[rows - HIST_ROWS:]
    pooled = pooled_parts[0] if nb == 1 else jnp.concatenate(pooled_parts, axis=0)
    mixed = (_dot(pooled.astype(_BF16), pw_ref[...]) + pb_ref[...]) * ps_ref[...]
    mp_ref[...] = (g_pool * _sigmoid(g_pool) * mixed).astype(_BF16)

    c0 = 2 * pw
    q = _dot(xb, w_ref[:, c0:c0 + aw]) * Q_SCALE
    k = _dot(xb, w_ref[:, c0 + aw:c0 + 2 * aw])
    v = _dot(xb, w_ref[:, c0 + 2 * aw:c0 + 3 * aw])
    g_attn = _dot(xb, w_ref[:, c0 + 3 * aw:c0 + 4 * aw])
    krow_ref[...] = k
    vrow_ref[...] = v
    sg_ref[...] = (g_attn * _sigmoid(g_attn)).astype(_BF16)
    for b in range(nb):
        for h in range(n_heads):
            rs, cs = slice(b * rows, (b + 1) * rows), slice(h * V_DIM, (h + 1) * V_DIM)
            k_ref[b, h] = k[rs, cs].astype(_BF16)
            if transposed:
                q_ref[b, h] = q[rs, cs].T.astype(_BF16)
                v_ref[b, h, 0:V_DIM, :] = v[rs, cs].T.astype(_BF16)
                ones_row = lax.broadcasted_iota(jnp.int32, (BF16_ROW_TILE, rows), 0) == 0
                v_ref[b, h, V_DIM:VT_ROWS, :] = ones_row.astype(_BF16)
            else:
                q_ref[b, h] = q[rs, cs].astype(_BF16)
                v_ref[b, h] = v[rs, cs].astype(_BF16)


def _in_proj(x2d, hist, w_in_b, pw_bd, pool_b, pool_scale, *, n_seq, seq_len, nb, rows, n_valid,
             transposed):
    d_model = x2d.shape[1]
    pw = pw_bd.shape[0]
    aw = (w_in_b.shape[1] - 2 * pw) // 4
    n_heads = aw // V_DIM
    assert n_seq % nb == 0 and seq_len % rows == 0 and rows >= HIST_ROWS and rows % 8 == 0
    assert nb == 1 or rows == seq_len
    n_tiles = seq_len // rows
    tm = nb * rows
    grid = (n_seq // nb, n_tiles)
    row_map = lambda s, i: (s * n_tiles + i, 0)
    const2 = lambda s, i: (0, 0)
    hm_shape = jax.ShapeDtypeStruct((n_seq, n_heads, seq_len, V_DIM), _BF16)
    hm_spec = pl.BlockSpec((nb, n_heads, rows, V_DIM), lambda s, i: (s, 0, i, 0))
    if transposed:
        q_shape = jax.ShapeDtypeStruct((n_seq, n_heads, V_DIM, seq_len), _BF16)
        q_spec = pl.BlockSpec((nb, n_heads, V_DIM, rows), lambda s, i: (s, 0, 0, i))
        v_shape = jax.ShapeDtypeStruct((n_seq, n_heads, VT_ROWS, seq_len), _BF16)
        v_spec = pl.BlockSpec((nb, n_heads, VT_ROWS, rows), lambda s, i: (s, 0, 0, i))
    else:
        q_shape, q_spec, v_shape, v_spec = hm_shape, hm_spec, hm_shape, hm_spec
    n_rows = n_seq * seq_len
    kernel = functools.partial(_proj_kernel, nb=nb, rows=rows, n_valid=n_valid, pool_width=pw,
                               attn_width=aw, n_heads=n_heads, transposed=transposed)
    return pl.pallas_call(
        kernel,
        grid=grid,
        in_specs=[
            pl.BlockSpec((tm, d_model), row_map),
            pl.BlockSpec((nb, HIST_ROWS, pw), lambda s, i: (s, 0, 0)),
            pl.BlockSpec(w_in_b.shape, const2),
            pl.BlockSpec(pw_bd.shape, const2),
            pl.BlockSpec((1, pw), const2),
            pl.BlockSpec((1, pw), const2),
        ],
        out_specs=[
            pl.BlockSpec((tm, aw), row_map),
            pl.BlockSpec((tm, aw), row_map),
            q_spec, hm_spec, v_spec,
            pl.BlockSpec((tm, aw), row_map),
            pl.BlockSpec((tm, pw), row_map),
            pl.BlockSpec((nb, HIST_ROWS, pw), lambda s, i: (s, 0, 0)),
        ],
        out_shape=[
            jax.ShapeDtypeStruct((n_rows, aw), _F32),
            jax.ShapeDtypeStruct((n_rows, aw), _F32),
            q_shape, hm_shape, v_shape,
            jax.ShapeDtypeStruct((n_rows, aw), _BF16),
            jax.ShapeDtypeStruct((n_rows, pw), _BF16),
            jax.ShapeDtypeStruct((n_seq, HIST_ROWS, pw), _F32),
        ],
        scratch_shapes=[pltpu.VMEM((nb, HIST_ROWS, pw), _F32)],
        compiler_params=pltpu.CompilerParams(
            dimension_semantics=("arbitrary", "arbitrary"),
            vmem_limit_bytes=V7X_VMEM_LIMIT_BYTES),
        name="in_proj_pool",
    )(x2d, hist, w_in_b, pw_bd, pool_b, pool_scale)


def _lambda_full(lam_ref, lam_init):
    lp = lam_ref[...]
    a1 = jnp.sum(lp[0:1] * lp[1:2], axis=1, keepdims=True)
    a2 = jnp.sum(lp[2:3] * lp[3:4], axis=1, keepdims=True)
    return jnp.exp(a1) - jnp.exp(a2) + lam_init


def _split_maps(q):
    lane = lax.broadcasted_iota(jnp.int32, q.shape, 1)
    zero = jnp.zeros_like(q)
    return jnp.where(lane < HEAD_DIM, q, zero), jnp.where(lane >= HEAD_DIM, q, zero)


def _finish_heads(o1, l1, o2, l2, lam, lam_init, g, sg):
    o = o1 / l1 - lam * (o2 / l2)
    o = o * lax.rsqrt(jnp.mean(o * o, axis=-1, keepdims=True) + SUBLN_EPS)
    o = o * g * (1.0 - lam_init)
    return (sg.astype(_F32) * o).astype(_BF16)


def _sublane_all(op, x):
    for shift in (4, 2, 1):
        x = op(x, pltpu.roll(x, shift=shift, axis=0))
    return x


def _flash_kernel(qT_ref, k_ref, vT_ref, sg_ref, lam_ref, g_ref, o_ref,
                  qz_ref, s0_ref, s1_ref, mx0_ref, mx1_ref, m_ref, acc_ref,
                  *, tq, tk, sub, lam_init):
    qi = pl.program_id(2)
    qT = qT_ref[0, 0]
    row = lax.broadcasted_iota(jnp.int32, qT.shape, 0)
    zero = jnp.zeros_like(qT)
    qz_ref[0] = jnp.where(row < HEAD_DIM, qT, zero)
    qz_ref[1] = jnp.where(row >= HEAD_DIM, qT, zero)
    m_ref[...] = jnp.full(m_ref.shape, -jnp.inf, _F32)
    acc_ref[...] = jnp.zeros(acc_ref.shape, _F32)
    s_bufs, mx_bufs = (s0_ref, s1_ref), (mx0_ref, mx1_ref)

    n_sub = tk // sub
    rows_of = lambda i: slice(i * sub, (i + 1) * sub)

    def scores_sub(j, slot, masked, i, mx):
        start = pl.multiple_of(j * tk + i * sub, sub)
        k = k_ref[0, 0, pl.ds(start, sub), :]
        if masked:
            k_pos = start + lax.broadcasted_iota(jnp.int32, (sub, tq), 0)
            q_pos = qi * tq + lax.broadcasted_iota(jnp.int32, (sub, tq), 1)
            visible = k_pos <= (q_pos | (CHUNK - 1))
        out = []
        for c in range(2):
            sT = _dot(k, qz_ref[c])
            if masked:
                sT = jnp.where(visible, sT, -jnp.inf)
            s_bufs[slot][c, rows_of(i), :] = sT
            mx_c = jnp.max(sT.reshape(sub // 8, 8, tq), axis=0)
            out.append(mx_c if mx is None else jnp.maximum(mx[c], mx_c))
        return out

    def pv_sub(j, slot, i, m_new, pv):
        start = pl.multiple_of(j * tk + i * sub, sub)
        vT = vT_ref[0, 0, :, pl.ds(start, sub)]
        out = []
        for c in range(2):
            s3 = s_bufs[slot][c, rows_of(i), :].reshape(sub // 8, 8, tq)
            p = jnp.exp2(s3 - m_new[c][None]).reshape(sub, tq).astype(_BF16)
            pv_c = _dot(vT, p)
            out.append(pv_c if pv is None else pv[c] + pv_c)
        return out

    def step(t, slot, do_pv, scores_masked):
        if do_pv:
            m_prev = [m_ref[c] for c in range(2)]
            m_new = [jnp.maximum(m_prev[c], _sublane_all(jnp.maximum, mx_bufs[slot][c]))
                     for c in range(2)]
        pv = mx = None
        for i in range(n_sub):
            if scores_masked is not None:
                mx = scores_sub(t + 1, 1 - slot, scores_masked, i, mx)
            if do_pv:
                pv = pv_sub(t, slot, i, m_new, pv)
        for c in range(2):
            if do_pv:
                alpha = jnp.exp2(m_prev[c] - m_new[c])
                acc_ref[c] = alpha[None] * acc_ref[c] + pv[c].reshape(VT_ROWS // 8, 8, tq)
                m_ref[c] = m_new[c]
            if scores_masked is not None:
                mx_bufs[1 - slot][c] = mx[c]

    n_tiles = 2 * qi + 2
    step(-1, 1, False, True)

    def pair(p, carry):
        step(2 * p, 0, True, False)
        step(2 * p + 1, 1, True, False)
        return carry

    lax.fori_loop(0, qi - 1, pair, 0)

    @pl.when(qi > 0)
    def _():
        step(n_tiles - 4, 0, True, False)
        step(n_tiles - 3, 1, True, True)

    step(n_tiles - 2, 0, True, True)
    step(n_tiles - 1, 1, True, None)

    lam = _lambda_full(lam_ref, lam_init).reshape(1, 1, 1)
    n_v = V_DIM // 8
    l1 = _sublane_all(jnp.add, acc_ref[0, n_v])
    l2 = _sublane_all(jnp.add, acc_ref[1, n_v])
    oT = acc_ref[0, 0:n_v] / l1[None] - lam * (acc_ref[1, 0:n_v] / l2[None])
    ms = _sublane_all(jnp.add, jnp.sum(oT * oT, axis=0)) * (1.0 / V_DIM)
    oT = oT * lax.rsqrt(ms + SUBLN_EPS)[None]
    oT = oT.reshape(V_DIM, tq) * g_ref[...] * (1.0 - lam_init)
    o_ref[0] = (sg_ref[0].astype(_F32) * oT.T).astype(_BF16)


def _flash_attention(qT_hm, k_hm, vT_hm, sg, lam_p, subln_g_col, *, tq, lam_init):
    n_seq, n_heads, seq_len, _ = k_hm.shape
    tk = tq // 2
    assert seq_len % tq == 0 and tk % CHUNK == 0 and tk % 128 == 0
    row_spec = pl.BlockSpec((1, tq, V_DIM), lambda b, h, i: (b, i, h))
    return pl.pallas_call(
        functools.partial(_flash_kernel, tq=tq, tk=tk, sub=tk, lam_init=lam_init),
        grid=(n_seq, n_heads, seq_len // tq),
        in_specs=[
            pl.BlockSpec((1, 1, V_DIM, tq), lambda b, h, i: (b, h, 0, i)),
            pl.BlockSpec((1, 1, seq_len, V_DIM), lambda b, h, i: (b, h, 0, 0)),
            pl.BlockSpec((1, 1, VT_ROWS, seq_len), lambda b, h, i: (b, h, 0, 0)),
            row_spec,
            pl.BlockSpec(lam_p.shape, lambda b, h, i: (0, 0)),
            pl.BlockSpec(subln_g_col.shape, lambda b, h, i: (0, 0)),
        ],
        out_specs=row_spec,
        out_shape=jax.ShapeDtypeStruct(sg.shape, _BF16),
        scratch_shapes=[pltpu.VMEM((2, V_DIM, tq), _BF16),
                        pltpu.VMEM((2, tk, tq), _F32), pltpu.VMEM((2, tk, tq), _F32),
                        pltpu.VMEM((2, 8, tq), _F32), pltpu.VMEM((2, 8, tq), _F32),
                        pltpu.VMEM((2, 8, tq), _F32),
                        pltpu.VMEM((2, VT_ROWS // 8, 8, tq), _F32)],
        compiler_params=pltpu.CompilerParams(
            dimension_semantics=("arbitrary", "arbitrary", "arbitrary"),
            vmem_limit_bytes=V7X_VMEM_LIMIT_BYTES),
        name="flash_diff_attn",
    )(qT_hm, k_hm, vT_hm, sg, lam_p, subln_g_col)


def _decode_attn_kernel(q_ref, k_ref, v_ref, ck_ref, cv_ref, sg_ref, lam_ref, g_ref, o_ref,
                        *, n_heads, vis_past, vis_new, lam_init):
    lam = _lambda_full(lam_ref, lam_init)
    for h in range(n_heads):
        cs = slice(h * V_DIM, (h + 1) * V_DIM)
        qs = _split_maps(q_ref[0, h])
        k_new, v_new = k_ref[0, h], v_ref[0, h]
        k_past = ck_ref[0, :, cs].astype(_BF16)
        v_past = cv_ref[0, :, cs].astype(_BF16)
        outs = []
        for c in range(2):
            s_past = _dot_nt(qs[c], k_past)
            s_new = _dot_nt(qs[c], k_new)
            if vis_past is not None:
                s_past = jnp.where(vis_past, s_past, -jnp.inf)
            if vis_new is not None:
                s_new = jnp.where(vis_new, s_new, -jnp.inf)
            m = jnp.maximum(jnp.max(s_past, axis=1, keepdims=True),
                            jnp.max(s_new, axis=1, keepdims=True))
            p_past = jnp.exp2(s_past - m)
            p_new = jnp.exp2(s_new - m)
            l = jnp.sum(p_past, axis=1, keepdims=True) + jnp.sum(p_new, axis=1, keepdims=True)
            o = _dot(p_past.astype(_BF16), v_past) + _dot(p_new.astype(_BF16), v_new)
            outs += [o, l]
        o_ref[0, :, cs] = _finish_heads(*outs, lam, lam_init, g_ref[...], sg_ref[0, :, cs])


def _static_visibility(q_pos, k_pos):
    vis = (k_pos[None, :] // CHUNK) <= (q_pos[:, None] // CHUNK)
    return None if vis.all() else jnp.asarray(vis)


def _decode_attention(q_hm, k_hm, v_hm, ck, cv, sg, lam_p, subln_g, *, lam_init):
    n_seq, n_heads, t_new, _ = q_hm.shape
    past = ck.shape[1]
    q_pos = past + np.arange(t_new)
    vis_past = _static_visibility(q_pos, np.arange(past))
    vis_new = _static_visibility(q_pos, q_pos)
    new_spec = pl.BlockSpec((1, n_heads, t_new, V_DIM), lambda b: (b, 0, 0, 0))
    cache_spec = pl.BlockSpec((1, past, n_heads * V_DIM), lambda b: (b, 0, 0))
    row_spec = pl.BlockSpec((1, t_new, n_heads * V_DIM), lambda b: (b, 0, 0))
    return pl.pallas_call(
        functools.partial(_decode_attn_kernel, n_heads=n_heads, vis_past=vis_past, vis_new=vis_new,
                          lam_init=lam_init),
        grid=(n_seq,),
        in_specs=[new_spec, new_spec, new_spec, cache_spec, cache_spec, row_spec,
                  pl.BlockSpec(lam_p.shape, lambda b: (0, 0)),
                  pl.BlockSpec(subln_g.shape, lambda b: (0, 0))],
        out_specs=row_spec,
        out_shape=jax.ShapeDtypeStruct(sg.shape, _BF16),
        compiler_params=pltpu.CompilerParams(
            dimension_semantics=("arbitrary",), vmem_limit_bytes=V7X_VMEM_LIMIT_BYTES),
        name="decode_diff_attn",
    )(q_hm, k_hm, v_hm, ck, cv, sg, lam_p, subln_g)


def _out_kernel(x_ref, mp_ref, ma_ref, w_ref, g_ref, b_ref, y_ref, *, pool_width, alpha):
    out = _dot(mp_ref[...], w_ref[0:pool_width, :]) + _dot(ma_ref[...], w_ref[pool_width:, :])
    z = alpha * x_ref[...] + out
    mu = jnp.mean(z, axis=-1, keepdims=True)
    zc = z - mu
    var = jnp.mean(zc * zc, axis=-1, keepdims=True)
    y_ref[...] = zc * lax.rsqrt(var + LN_EPS) * g_ref[...] + b_ref[...]


def _out_proj(x2d, mp, ma, w_out_b, ln_g, ln_b, *, tm, alpha):
    n_rows, d_model = x2d.shape
    pw, aw = mp.shape[1], ma.shape[1]
    assert n_rows % tm == 0
    row = lambda i: (i, 0)
    const = lambda i: (0, 0)
    return pl.pallas_call(
        functools.partial(_out_kernel, pool_width=pw, alpha=alpha),
        grid=(n_rows // tm,),
        in_specs=[pl.BlockSpec((tm, d_model), row), pl.BlockSpec((tm, pw), row),
                  pl.BlockSpec((tm, aw), row), pl.BlockSpec(w_out_b.shape, const),
                  pl.BlockSpec((1, d_model), const), pl.BlockSpec((1, d_model), const)],
        out_specs=pl.BlockSpec((tm, d_model), row),
        out_shape=jax.ShapeDtypeStruct((n_rows, d_model), _F32),
        compiler_params=pltpu.CompilerParams(
            dimension_semantics=("arbitrary",), vmem_limit_bytes=V7X_VMEM_LIMIT_BYTES),
        name="out_proj_ln",
    )(x2d, mp, ma, w_out_b, ln_g, ln_b)


def _block_diag(pool_w):
    g, c, d = pool_w.shape
    out = jnp.zeros((g * c, g * d), pool_w.dtype)
    for i in range(g):
        out = out.at[i * c:(i + 1) * c, i * d:(i + 1) * d].set(pool_w[i])
    return out


def _layer(x, hist, n_valid, k_past, v_past, lam_init, alpha, params, *, rows, tq, tm_out):
    (w_in, pool_w, pool_b, pool_scale, lq1, lk1, lq2, lk2, subln_g, w_out, ln_g, ln_b) = params
    n_seq, seq_len, d_model = x.shape
    pw = pool_scale.shape[0]
    aw = w_out.shape[0] - pw
    n_heads = aw // V_DIM
    x2d = x.reshape(n_seq * seq_len, d_model)
    w_in_b, w_out_b = w_in.astype(_BF16), w_out.astype(_BF16)
    pw_bd = _block_diag(pool_w).astype(_BF16)
    lam_p = jnp.stack([lq1, lk1, lq2, lk2]).astype(_F32)
    prompt = k_past is None
    nb = 1 if prompt else n_seq
    krow, vrow, q_hm, k_hm, v_hm, sg, mp, pst = _in_proj(
        x2d, hist, w_in_b, pw_bd, pool_b.reshape(1, pw).astype(_F32),
        pool_scale.reshape(1, pw).astype(_F32),
        n_seq=n_seq, seq_len=seq_len, nb=nb, rows=rows, n_valid=n_valid, transposed=prompt)
    sg3 = sg.reshape(n_seq, seq_len, aw)
    if prompt:
        ma = _flash_attention(q_hm, k_hm, v_hm, sg3, lam_p, subln_g.reshape(V_DIM, 1).astype(_F32),
                              tq=tq, lam_init=lam_init)
    else:
        past = k_past.shape[1]
        ma = _decode_attention(q_hm, k_hm, v_hm, k_past.reshape(n_seq, past, aw),
                               v_past.reshape(n_seq, past, aw), sg3, lam_p,
                               subln_g.reshape(1, V_DIM).astype(_F32), lam_init=lam_init)
    y = _out_proj(x2d, mp, ma.reshape(n_seq * seq_len, aw), w_out_b,
                  ln_g.reshape(1, d_model).astype(_F32), ln_b.reshape(1, d_model).astype(_F32),
                  tm=tm_out, alpha=alpha)
    k_rows = krow.reshape(n_seq, seq_len, n_heads, V_DIM)
    v_rows = vrow.reshape(n_seq, seq_len, n_heads, V_DIM)
    return y.reshape(n_seq, seq_len, d_model), k_rows, v_rows, pst


def kernel(x_prompt, x_sample, cache_k, cache_v, state_pool, w_in, pool_w, pool_b, pool_scale,
           lambda_q1, lambda_k1, lambda_q2, lambda_k2, subln_g, w_out, ln_g, ln_b):
    depth = w_in.shape[0]
    pw = pool_scale.shape[1]
    pool_hist = max(POOL_WINDOWS) - 1
    alpha = (2.0 * depth) ** 0.25
    n_valid_sample = min(cache_k.shape[2], pool_hist)
    yp, ys = x_prompt, x_sample
    outs = [[] for _ in range(6)]
    for l in range(depth):
        lam_init = _lambda_init(l)
        params = (w_in[l], pool_w[l], pool_b[l], pool_scale[l], lambda_q1[l], lambda_k1[l],
                  lambda_q2[l], lambda_k2[l], subln_g[l], w_out[l], ln_g[l], ln_b[l])
        zero_hist = jnp.zeros((yp.shape[0], HIST_ROWS, pw), _F32)
        yp, kp, vp, pp = _layer(yp, zero_hist, 0, None, None, lam_init, alpha, params,
                                rows=512, tq=1024, tm_out=512)
        hist_s = jnp.pad(state_pool[l].astype(_F32), ((0, 0), (HIST_ROWS - pool_hist, 0), (0, 0)))
        ys, kn, vn, pn = _layer(ys, hist_s, n_valid_sample, cache_k[l], cache_v[l], lam_init, alpha,
                                params, rows=ys.shape[1], tq=None,
                                tm_out=ys.shape[0] * ys.shape[1])
        for lst, val in zip(outs, (kp, vp, pp[:, HIST_ROWS - pool_hist:], kn, vn,
                                   pn[:, HIST_ROWS - pool_hist:])):
            lst.append(val)
    return (yp, ys) + tuple(jnp.stack(o) for o in outs)
```

```python
import functools
import math

import numpy as np
import jax
import jax.numpy as jnp
from jax import lax
from jax.experimental import pallas as pl
from jax.experimental.pallas import tpu as pltpu

CHUNK = 64
POOL_WINDOWS = (2, 4, 8, 16)
POOL_GROUPS = len(POOL_WINDOWS)
HEAD_DIM = 64
V_DIM = 2 * HEAD_DIM
ATTN_SCALE = HEAD_DIM ** -0.5
Q_SCALE = ATTN_SCALE * math.log2(math.e)
LN_EPS = 1e-5
SUBLN_EPS = 1e-5

BF16_ROW_TILE = 16
VT_ROWS = V_DIM + BF16_ROW_TILE

HIST_ROWS = 16
V7X_VMEM_LIMIT_BYTES = 56 * 1024 * 1024

_BF16 = jnp.bfloat16
_F32 = jnp.float32


def _lambda_init(layer_idx):
    return 0.8 - 0.6 * math.exp(-0.3 * layer_idx)


def _sigmoid(x):
    return 1.0 / (1.0 + jnp.exp(-x))


def _dot(a, b):
    return jnp.dot(a, b, preferred_element_type=_F32)


def _dot_nt(a, b):
    return lax.dot_general(a, b, (((1,), (1,)), ((), ())), preferred_element_type=_F32)


def _proj_kernel(x_ref, hist_ref, w_ref, pw_ref, pb_ref, ps_ref,
                 krow_ref, vrow_ref, q_ref, k_ref, v_ref, sg_ref, mp_ref, pst_ref,
                 carry_ref, *, nb, rows, n_valid, pool_width, attn_width, n_heads, transposed):
    i = pl.program_id(1)
    pw, aw = pool_width, attn_width
    group_dim = pw // POOL_GROUPS

    @pl.when(i == 0)
    def _():
        carry_ref[...] = hist_ref[...]

    xb = x_ref[...].astype(_BF16)
    u = _dot(xb, w_ref[:, 0:pw])
    g_pool = _dot(xb, w_ref[:, pw:2 * pw])

    ext_rows = HIST_ROWS + rows
    lane = lax.broadcasted_iota(jnp.int32, (ext_rows, pw), 1)
    row = lax.broadcasted_iota(jnp.int32, (ext_rows, pw), 0)
    t_pos = i * rows + row - HIST_ROWS
    win = jnp.full((ext_rows, pw), POOL_WINDOWS[-1], jnp.int32)
    for g in range(POOL_GROUPS - 2, -1, -1):
        win = jnp.where(lane < (g + 1) * group_dim, POOL_WINDOWS[g], win)
    cnt = jnp.maximum(jnp.minimum(win, n_valid + t_pos + 1), 1).astype(_F32)
    pooled_parts = []
    for b in range(nb):
        u_b = u[b * rows:(b + 1) * rows]
        ext = jnp.concatenate([carry_ref[b], u_b], axis=0)
        sums = {1: ext}
        w = 1
        while w < POOL_WINDOWS[-1]:
            sums[2 * w] = sums[w] + pltpu.roll(sums[w], shift=w, axis=0)
            w *= 2
        sel = sums[POOL_WINDOWS[-1]]
        for g in range(POOL_GROUPS - 2, -1, -1):
            sel = jnp.where(lane < (g + 1) * group_dim, sums[POOL_WINDOWS[g]], sel)
        pooled_parts.append((sel / cnt - ext)[HIST_ROWS:])
        carry_ref[b] = u_b[rows - HIST_ROWS:]
        pst_ref[b] = u_b[rows - HIST_ROWS:]
    pooled = pooled_parts[0] if nb == 1 else jnp.concatenate(pooled_parts, axis=0)
    mixed = (_dot(pooled.astype(_BF16), pw_ref[...]) + pb_ref[...]) * ps_ref[...]
    mp_ref[...] = (g_pool * _sigmoid(g_pool) * mixed).astype(_BF16)

    c0 = 2 * pw
    q = _dot(xb, w_ref[:, c0:c0 + aw]) * Q_SCALE
    k = _dot(xb, w_ref[:, c0 + aw:c0 + 2 * aw])
    v = _dot(xb, w_ref[:, c0 + 2 * aw:c0 + 3 * aw])
    g_attn = _dot(xb, w_ref[:, c0 + 3 * aw:c0 + 4 * aw])
    sg_ref[...] = (g_attn * _sigmoid(g_attn)).astype(_BF16)
    for b in range(nb):
        for h in range(n_heads):
            rs, cs = slice(b * rows, (b + 1) * rows), slice(h * V_DIM, (h + 1) * V_DIM)
            krow_ref[b, :, h, :] = k[rs, cs]
            vrow_ref[b, :, h, :] = v[rs, cs]
            k_ref[b, h] = k[rs, cs].astype(_BF16)
            if transposed:
                q_ref[b, h] = q[rs, cs].T.astype(_BF16)
                v_ref[b, h, 0:V_DIM, :] = v[rs, cs].T.astype(_BF16)
                ones_row = lax.broadcasted_iota(jnp.int32, (BF16_ROW_TILE, rows), 0) == 0
                v_ref[b, h, V_DIM:VT_ROWS, :] = ones_row.astype(_BF16)
            else:
                q_ref[b, h] = q[rs, cs].astype(_BF16)
                v_ref[b, h] = v[rs, cs].astype(_BF16)


def _in_proj(x2d, hist, w_in_b, pw_bd, pool_b, pool_scale, *, n_seq, seq_len, nb, rows, n_valid,
             transposed):
    d_model = x2d.shape[1]
    pw = pw_bd.shape[0]
    aw = (w_in_b.shape[1] - 2 * pw) // 4
    n_heads = aw // V_DIM
    assert n_seq % nb == 0 and seq_len % rows == 0 and rows >= HIST_ROWS and rows % 8 == 0
    assert nb == 1 or rows == seq_len
    n_tiles = seq_len // rows
    tm = nb * rows
    grid = (n_seq // nb, n_tiles)
    row_map = lambda s, i: (s * n_tiles + i, 0)
    const2 = lambda s, i: (0, 0)
    hm_shape = jax.ShapeDtypeStruct((n_seq, n_heads, seq_len, V_DIM), _BF16)
    hm_spec = pl.BlockSpec((nb, n_heads, rows, V_DIM), lambda s, i: (s, 0, i, 0))
    if transposed:
        q_shape = jax.ShapeDtypeStruct((n_seq, n_heads, V_DIM, seq_len), _BF16)
        q_spec = pl.BlockSpec((nb, n_heads, V_DIM, rows), lambda s, i: (s, 0, 0, i))
        v_shape = jax.ShapeDtypeStruct((n_seq, n_heads, VT_ROWS, seq_len), _BF16)
        v_spec = pl.BlockSpec((nb, n_heads, VT_ROWS, rows), lambda s, i: (s, 0, 0, i))
    else:
        q_shape, q_spec, v_shape, v_spec = hm_shape, hm_spec, hm_shape, hm_spec
    n_rows = n_seq * seq_len
    rows_shape = jax.ShapeDtypeStruct((n_seq, seq_len, n_heads, V_DIM), _F32)
    rows_spec = pl.BlockSpec((nb, rows, n_heads, V_DIM), lambda s, i: (s, i, 0, 0))
    kernel = functools.partial(_proj_kernel, nb=nb, rows=rows, n_valid=n_valid, pool_width=pw,
                               attn_width=aw, n_heads=n_heads, transposed=transposed)
    return pl.pallas_call(
        kernel,
        grid=grid,
        in_specs=[
            pl.BlockSpec((tm, d_model), row_map),
            pl.BlockSpec((nb, HIST_ROWS, pw), lambda s, i: (s, 0, 0)),
            pl.BlockSpec(w_in_b.shape, const2),
            pl.BlockSpec(pw_bd.shape, const2),
            pl.BlockSpec((1, pw), const2),
            pl.BlockSpec((1, pw), const2),
        ],
        out_specs=[
            rows_spec, rows_spec,
            q_spec, hm_spec, v_spec,
            pl.BlockSpec((tm, aw), row_map),
            pl.BlockSpec((tm, pw), row_map),
            pl.BlockSpec((nb, HIST_ROWS, pw), lambda s, i: (s, 0, 0)),
        ],
        out_shape=[
            rows_shape, rows_shape,
            q_shape, hm_shape, v_shape,
            jax.ShapeDtypeStruct((n_rows, aw), _BF16),
            jax.ShapeDtypeStruct((n_rows, pw), _BF16),
            jax.ShapeDtypeStruct((n_seq, HIST_ROWS, pw), _F32),
        ],
        scratch_shapes=[pltpu.VMEM((nb, HIST_ROWS, pw), _F32)],
        compiler_params=pltpu.CompilerParams(
            dimension_semantics=("arbitrary", "arbitrary"),
            vmem_limit_bytes=V7X_VMEM_LIMIT_BYTES),
        name="in_proj_pool",
    )(x2d, hist, w_in_b, pw_bd, pool_b, pool_scale)


def _lambda_full(lam_ref, lam_init):
    lp = lam_ref[...]
    a1 = jnp.sum(lp[0:1] * lp[1:2], axis=1, keepdims=True)
    a2 = jnp.sum(lp[2:3] * lp[3:4], axis=1, keepdims=True)
    return jnp.exp(a1) - jnp.exp(a2) + lam_init


def _split_maps(q):
    lane = lax.broadcasted_iota(jnp.int32, q.shape, 1)
    zero = jnp.zeros_like(q)
    return jnp.where(lane < HEAD_DIM, q, zero), jnp.where(lane >= HEAD_DIM, q, zero)


def _finish_heads(o1, l1, o2, l2, lam, lam_init, g, sg):
    o = o1 / l1 - lam * (o2 / l2)
    o = o * lax.rsqrt(jnp.mean(o * o, axis=-1, keepdims=True) + SUBLN_EPS)
    o = o * g * (1.0 - lam_init)
    return (sg.astype(_F32) * o).astype(_BF16)


def _sublane_all(op, x):
    for shift in (4, 2, 1):
        x = op(x, pltpu.roll(x, shift=shift, axis=0))
    return x


def _flash_kernel(qT_ref, k_ref, vT_ref, sg_ref, lam_ref, g_ref, o_ref,
                  qz_ref, s0_ref, s1_ref, mx0_ref, mx1_ref, m_ref, acc_ref,
                  *, tq, tk, sub, lam_init):
    qi = pl.program_id(2)
    qT = qT_ref[0, 0]
    row = lax.broadcasted_iota(jnp.int32, qT.shape, 0)
    zero = jnp.zeros_like(qT)
    qz_ref[0] = jnp.where(row < HEAD_DIM, qT, zero)
    qz_ref[1] = jnp.where(row >= HEAD_DIM, qT, zero)
    m_ref[...] = jnp.full(m_ref.shape, -jnp.inf, _F32)
    acc_ref[...] = jnp.zeros(acc_ref.shape, _F32)
    s_bufs, mx_bufs = (s0_ref, s1_ref), (mx0_ref, mx1_ref)

    n_sub = tk // sub
    rows_of = lambda i: slice(i * sub, (i + 1) * sub)

    def scores_sub(j, slot, masked, i, mx):
        start = pl.multiple_of(j * tk + i * sub, sub)
        k = k_ref[0, 0, pl.ds(start, sub), :]
        if masked:
            k_pos = start + lax.broadcasted_iota(jnp.int32, (sub, tq), 0)
            q_pos = qi * tq + lax.broadcasted_iota(jnp.int32, (sub, tq), 1)
            visible = k_pos <= (q_pos | (CHUNK - 1))
        out = []
        for c in range(2):
            sT = _dot(k, qz_ref[c])
            if masked:
                sT = jnp.where(visible, sT, -jnp.inf)
            s_bufs[slot][c, rows_of(i), :] = sT
            mx_c = jnp.max(sT.reshape(sub // 8, 8, tq), axis=0)
            out.append(mx_c if mx is None else jnp.maximum(mx[c], mx_c))
        return out

    def pv_sub(j, slot, i, m_new, pv):
        start = pl.multiple_of(j * tk + i * sub, sub)
        vT = vT_ref[0, 0, :, pl.ds(start, sub)]
        out = []
        for c in range(2):
            s3 = s_bufs[slot][c, rows_of(i), :].reshape(sub // 8, 8, tq)
            p = jnp.exp2(s3 - m_new[c][None]).reshape(sub, tq).astype(_BF16)
            pv_c = _dot(vT, p)
            out.append(pv_c if pv is None else pv[c] + pv_c)
        return out

    def step(t, slot, do_pv, scores_masked):
        if do_pv:
            m_prev = [m_ref[c] for c in range(2)]
            m_new = [jnp.maximum(m_prev[c], _sublane_all(jnp.maximum, mx_bufs[slot][c]))
                     for c in range(2)]
        pv = mx = None
        for i in range(n_sub):
            if scores_masked is not None:
                mx = scores_sub(t + 1, 1 - slot, scores_masked, i, mx)
            if do_pv:
                pv = pv_sub(t, slot, i, m_new, pv)
        for c in range(2):
            if do_pv:
                alpha = jnp.exp2(m_prev[c] - m_new[c])
                acc_ref[c] = alpha[None] * acc_ref[c] + pv[c].reshape(VT_ROWS // 8, 8, tq)
                m_ref[c] = m_new[c]
            if scores_masked is not None:
                mx_bufs[1 - slot][c] = mx[c]

    n_tiles = 2 * qi + 2
    step(-1, 1, False, True)

    def pair(p, carry):
        step(2 * p, 0, True, False)
        step(2 * p + 1, 1, True, False)
        return carry

    lax.fori_loop(0, qi - 1, pair, 0)

    @pl.when(qi > 0)
    def _():
        step(n_tiles - 4, 0, True, False)
        step(n_tiles - 3, 1, True, True)

    step(n_tiles - 2, 0, True, True)
    step(n_tiles - 1, 1, True, None)

    lam = _lambda_full(lam_ref, lam_init).reshape(1, 1, 1)
    n_v = V_DIM // 8
    l1 = _sublane_all(jnp.add, acc_ref[0, n_v])
    l2 = _sublane_all(jnp.add, acc_ref[1, n_v])
    oT = acc_ref[0, 0:n_v] / l1[None] - lam * (acc_ref[1, 0:n_v] / l2[None])
    ms = _sublane_all(jnp.add, jnp.sum(oT * oT, axis=0)) * (1.0 / V_DIM)
    oT = oT * lax.rsqrt(ms + SUBLN_EPS)[None]
    oT = oT.reshape(V_DIM, tq) * g_ref[...] * (1.0 - lam_init)
    o_ref[0] = (sg_ref[0].astype(_F32) * oT.T).astype(_BF16)


def _flash_attention(qT_hm, k_hm, vT_hm, sg, lam_p, subln_g_col, *, tq, lam_init):
    n_seq, n_heads, seq_len, _ = k_hm.shape
    tk = tq // 2
    assert seq_len % tq == 0 and tk % CHUNK == 0 and tk % 128 == 0
    row_spec = pl.BlockSpec((1, tq, V_DIM), lambda b, h, i: (b, i, h))
    return pl.pallas_call(
        functools.partial(_flash_kernel, tq=tq, tk=tk, sub=tk, lam_init=lam_init),
        grid=(n_seq, n_heads, seq_len // tq),
        in_specs=[
            pl.BlockSpec((1, 1, V_DIM, tq), lambda b, h, i: (b, h, 0, i)),
            pl.BlockSpec((1, 1, seq_len, V_DIM), lambda b, h, i: (b, h, 0, 0)),
            pl.BlockSpec((1, 1, VT_ROWS, seq_len), lambda b, h, i: (b, h, 0, 0)),
            row_spec,
            pl.BlockSpec(lam_p.shape, lambda b, h, i: (0, 0)),
            pl.BlockSpec(subln_g_col.shape, lambda b, h, i: (0, 0)),
        ],
        out_specs=row_spec,
        out_shape=jax.ShapeDtypeStruct(sg.shape, _BF16),
        scratch_shapes=[pltpu.VMEM((2, V_DIM, tq), _BF16),
                        pltpu.VMEM((2, tk, tq), _F32), pltpu.VMEM((2, tk, tq), _F32),
                        pltpu.VMEM((2, 8, tq), _F32), pltpu.VMEM((2, 8, tq), _F32),
                        pltpu.VMEM((2, 8, tq), _F32),
                        pltpu.VMEM((2, VT_ROWS // 8, 8, tq), _F32)],
        compiler_params=pltpu.CompilerParams(
            dimension_semantics=("arbitrary", "arbitrary", "arbitrary"),
            vmem_limit_bytes=V7X_VMEM_LIMIT_BYTES),
        name="flash_diff_attn",
    )(qT_hm, k_hm, vT_hm, sg, lam_p, subln_g_col)


def _decode_attn_kernel(q_ref, k_ref, v_ref, ck_ref, cv_ref, sg_ref, lam_ref, g_ref, o_ref,
                        *, n_heads, vis_past, vis_new, lam_init):
    lam = _lambda_full(lam_ref, lam_init)
    for h in range(n_heads):
        cs = slice(h * V_DIM, (h + 1) * V_DIM)
        qs = _split_maps(q_ref[0, h])
        k_new, v_new = k_ref[0, h], v_ref[0, h]
        k_past = ck_ref[0, :, h, :].astype(_BF16)
        v_past = cv_ref[0, :, h, :].astype(_BF16)
        outs = []
        for c in range(2):
            s_past = _dot_nt(qs[c], k_past)
            s_new = _dot_nt(qs[c], k_new)
            if vis_past is not None:
                s_past = jnp.where(vis_past, s_past, -jnp.inf)
            if vis_new is not None:
                s_new = jnp.where(vis_new, s_new, -jnp.inf)
            m = jnp.maximum(jnp.max(s_past, axis=1, keepdims=True),
                            jnp.max(s_new, axis=1, keepdims=True))
            p_past = jnp.exp2(s_past - m)
            p_new = jnp.exp2(s_new - m)
            l = jnp.sum(p_past, axis=1, keepdims=True) + jnp.sum(p_new, axis=1, keepdims=True)
            o = _dot(p_past.astype(_BF16), v_past) + _dot(p_new.astype(_BF16), v_new)
            outs += [o, l]
        o_ref[0, :, cs] = _finish_heads(*outs, lam, lam_init, g_ref[...], sg_ref[0, :, cs])


def _static_visibility(q_pos, k_pos):
    vis = (k_pos[None, :] // CHUNK) <= (q_pos[:, None] // CHUNK)
    return None if vis.all() else jnp.asarray(vis)


def _decode_attention(q_hm, k_hm, v_hm, ck, cv, sg, lam_p, subln_g, *, lam_init):
    n_seq, n_heads, t_new, _ = q_hm.shape
    past = ck.shape[1]
    q_pos = past + np.arange(t_new)
    vis_past = _static_visibility(q_pos, np.arange(past))
    vis_new = _static_visibility(q_pos, q_pos)
    new_spec = pl.BlockSpec((1, n_heads, t_new, V_DIM), lambda b: (b, 0, 0, 0))
    cache_spec = pl.BlockSpec((1, past, n_heads, V_DIM), lambda b: (b, 0, 0, 0))
    row_spec = pl.BlockSpec((1, t_new, n_heads * V_DIM), lambda b: (b, 0, 0))
    return pl.pallas_call(
        functools.partial(_decode_attn_kernel, n_heads=n_heads, vis_past=vis_past, vis_new=vis_new,
                          lam_init=lam_init),
        grid=(n_seq,),
        in_specs=[new_spec, new_spec, new_spec, cache_spec, cache_spec, row_spec,
                  pl.BlockSpec(lam_p.shape, lambda b: (0, 0)),
                  pl.BlockSpec(subln_g.shape, lambda b: (0, 0))],
        out_specs=row_spec,
        out_shape=jax.ShapeDtypeStruct(sg.shape, _BF16),
        compiler_params=pltpu.CompilerParams(
            dimension_semantics=("arbitrary",), vmem_limit_bytes=V7X_VMEM_LIMIT_BYTES),
        name="decode_diff_attn",
    )(q_hm, k_hm, v_hm, ck, cv, sg, lam_p, subln_g)


def _out_kernel(x_ref, mp_ref, ma_ref, w_ref, g_ref, b_ref, y_ref, *, pool_width, alpha):
    out = _dot(mp_ref[...], w_ref[0:pool_width, :]) + _dot(ma_ref[...], w_ref[pool_width:, :])
    z = alpha * x_ref[...] + out
    mu = jnp.mean(z, axis=-1, keepdims=True)
    zc = z - mu
    var = jnp.mean(zc * zc, axis=-1, keepdims=True)
    y_ref[...] = zc * lax.rsqrt(var + LN_EPS) * g_ref[...] + b_ref[...]


def _out_proj(x2d, mp, ma, w_out_b, ln_g, ln_b, *, tm, alpha):
    n_rows, d_model = x2d.shape
    pw, aw = mp.shape[1], ma.shape[1]
    assert n_rows % tm == 0
    row = lambda i: (i, 0)
    const = lambda i: (0, 0)
    return pl.pallas_call(
        functools.partial(_out_kernel, pool_width=pw, alpha=alpha),
        grid=(n_rows // tm,),
        in_specs=[pl.BlockSpec((tm, d_model), row), pl.BlockSpec((tm, pw), row),
                  pl.BlockSpec((tm, aw), row), pl.BlockSpec(w_out_b.shape, const),
                  pl.BlockSpec((1, d_model), const), pl.BlockSpec((1, d_model), const)],
        out_specs=pl.BlockSpec((tm, d_model), row),
        out_shape=jax.ShapeDtypeStruct((n_rows, d_model), _F32),
        compiler_params=pltpu.CompilerParams(
            dimension_semantics=("arbitrary",), vmem_limit_bytes=V7X_VMEM_LIMIT_BYTES),
        name="out_proj_ln",
    )(x2d, mp, ma, w_out_b, ln_g, ln_b)


def _block_diag(pool_w):
    g, c, d = pool_w.shape
    out = jnp.zeros((g * c, g * d), pool_w.dtype)
    for i in range(g):
        out = out.at[i * c:(i + 1) * c, i * d:(i + 1) * d].set(pool_w[i])
    return out


def _layer(x, hist, n_valid, k_past, v_past, lam_init, alpha, params, *, rows, tq, tm_out):
    (w_in, pool_w, pool_b, pool_scale, lq1, lk1, lq2, lk2, subln_g, w_out, ln_g, ln_b) = params
    n_seq, seq_len, d_model = x.shape
    pw = pool_scale.shape[0]
    aw = w_out.shape[0] - pw
    n_heads = aw // V_DIM
    x2d = x.reshape(n_seq * seq_len, d_model)
    w_in_b, w_out_b = w_in.astype(_BF16), w_out.astype(_BF16)
    pw_bd = _block_diag(pool_w).astype(_BF16)
    lam_p = jnp.stack([lq1, lk1, lq2, lk2]).astype(_F32)
    prompt = k_past is None
    nb = 1 if prompt else n_seq
    krow, vrow, q_hm, k_hm, v_hm, sg, mp, pst = _in_proj(
        x2d, hist, w_in_b, pw_bd, pool_b.reshape(1, pw).astype(_F32),
        pool_scale.reshape(1, pw).astype(_F32),
        n_seq=n_seq, seq_len=seq_len, nb=nb, rows=rows, n_valid=n_valid, transposed=prompt)
    sg3 = sg.reshape(n_seq, seq_len, aw)
    if prompt:
        ma = _flash_attention(q_hm, k_hm, v_hm, sg3, lam_p, subln_g.reshape(V_DIM, 1).astype(_F32),
                              tq=tq, lam_init=lam_init)
    else:
        ma = _decode_attention(q_hm, k_hm, v_hm, k_past, v_past, sg3, lam_p,
                               subln_g.reshape(1, V_DIM).astype(_F32), lam_init=lam_init)
    y = _out_proj(x2d, mp, ma.reshape(n_seq * seq_len, aw), w_out_b,
                  ln_g.reshape(1, d_model).astype(_F32), ln_b.reshape(1, d_model).astype(_F32),
                  tm=tm_out, alpha=alpha)
    return y.reshape(n_seq, seq_len, d_model), krow, vrow, pst


def kernel(x_prompt, x_sample, cache_k, cache_v, state_pool, w_in, pool_w, pool_b, pool_scale,
           lambda_q1, lambda_k1, lambda_q2, lambda_k2, subln_g, w_out, ln_g, ln_b):
    depth = w_in.shape[0]
    pw = pool_scale.shape[1]
    pool_hist = max(POOL_WINDOWS) - 1
    alpha = (2.0 * depth) ** 0.25
    n_valid_sample = min(cache_k.shape[2], pool_hist)
    yp, ys = x_prompt, x_sample
    outs = [[] for _ in range(6)]
    for l in range(depth):
        lam_init = _lambda_init(l)
        params = (w_in[l], pool_w[l], pool_b[l], pool_scale[l], lambda_q1[l], lambda_k1[l],
                  lambda_q2[l], lambda_k2[l], subln_g[l], w_out[l], ln_g[l], ln_b[l])
        zero_hist = jnp.zeros((yp.shape[0], HIST_ROWS, pw), _F32)
        yp, kp, vp, pp = _layer(yp, zero_hist, 0, None, None, lam_init, alpha, params,
                                rows=512, tq=1024, tm_out=512)
        hist_s = jnp.pad(state_pool[l].astype(_F32), ((0, 0), (HIST_ROWS - pool_hist, 0), (0, 0)))
        ys, kn, vn, pn = _layer(ys, hist_s, n_valid_sample, cache_k[l], cache_v[l], lam_init, alpha,
                                params, rows=ys.shape[1], tq=None,
                                tm_out=ys.shape[0] * ys.shape[1])
        for lst, val in zip(outs, (kp, vp, pp[:, HIST_ROWS - pool_hist:], kn, vn,
                                   pn[:, HIST_ROWS - pool_hist:])):
            lst.append(val)
    return (yp, ys) + tuple(jnp.stack(o) for o in outs)
```

```python
import functools
import math

import numpy as np
import jax
import jax.numpy as jnp
from jax import lax
from jax.experimental import pallas as pl
from jax.experimental.pallas import tpu as pltpu

CHUNK = 64
POOL_WINDOWS = (2, 4, 8, 16)
POOL_GROUPS = len(POOL_WINDOWS)
HEAD_DIM = 64
V_DIM = 2 * HEAD_DIM
ATTN_SCALE = HEAD_DIM ** -0.5
Q_SCALE = ATTN_SCALE * math.log2(math.e)
LN_EPS = 1e-5
SUBLN_EPS = 1e-5

BF16_ROW_TILE = 16
F32_SUBLANES = 8
VT_ROWS = V_DIM + BF16_ROW_TILE

HIST_ROWS = 16
V7X_VMEM_LIMIT_BYTES = 56 * 1024 * 1024

_BF16 = jnp.bfloat16
_F32 = jnp.float32


def _lambda_init(layer_idx):
    return 0.8 - 0.6 * math.exp(-0.3 * layer_idx)


def _sigmoid(x):
    return 1.0 / (1.0 + jnp.exp(-x))


def _dot(a, b):
    return jnp.dot(a, b, preferred_element_type=_F32)


def _dot_nt(a, b):
    return lax.dot_general(a, b, (((1,), (1,)), ((), ())), preferred_element_type=_F32)


def _proj_kernel(x_ref, hist_ref, w_ref, pw_ref, pb_ref, ps_ref,
                 krow_ref, vrow_ref, q_ref, k_ref, v_ref, sg_ref, mp_ref, pst_ref,
                 carry_ref, kstage_ref, vstage_ref,
                 *, nb, rows, n_valid, pool_width, attn_width, n_heads, transposed):
    i = pl.program_id(1)
    pw, aw = pool_width, attn_width
    group_dim = pw // POOL_GROUPS

    @pl.when(i == 0)
    def _():
        carry_ref[...] = hist_ref[...]

    xb = x_ref[...].astype(_BF16)
    u = _dot(xb, w_ref[:, 0:pw])
    g_pool = _dot(xb, w_ref[:, pw:2 * pw])

    ext_rows = HIST_ROWS + rows
    lane = lax.broadcasted_iota(jnp.int32, (ext_rows, pw), 1)
    row = lax.broadcasted_iota(jnp.int32, (ext_rows, pw), 0)
    t_pos = i * rows + row - HIST_ROWS
    win = jnp.full((ext_rows, pw), POOL_WINDOWS[-1], jnp.int32)
    for g in range(POOL_GROUPS - 2, -1, -1):
        win = jnp.where(lane < (g + 1) * group_dim, POOL_WINDOWS[g], win)
    cnt = jnp.maximum(jnp.minimum(win, n_valid + t_pos + 1), 1).astype(_F32)
    pooled_parts = []
    for b in range(nb):
        u_b = u[b * rows:(b + 1) * rows]
        ext = jnp.concatenate([carry_ref[b], u_b], axis=0)
        sums = {1: ext}
        w = 1
        while w < POOL_WINDOWS[-1]:
            sums[2 * w] = sums[w] + pltpu.roll(sums[w], shift=w, axis=0)
            w *= 2
        sel = sums[POOL_WINDOWS[-1]]
        for g in range(POOL_GROUPS - 2, -1, -1):
            sel = jnp.where(lane < (g + 1) * group_dim, sums[POOL_WINDOWS[g]], sel)
        pooled_parts.append((sel / cnt - ext)[HIST_ROWS:])
        carry_ref[b] = u_b[rows - HIST_ROWS:]
        pst_ref[b] = u_b[rows - HIST_ROWS:]
    pooled = pooled_parts[0] if nb == 1 else jnp.concatenate(pooled_parts, axis=0)
    mixed = (_dot(pooled.astype(_BF16), pw_ref[...]) + pb_ref[...]) * ps_ref[...]
    mp_ref[...] = (g_pool * _sigmoid(g_pool) * mixed).astype(_BF16)

    c0 = 2 * pw
    q = _dot(xb, w_ref[:, c0:c0 + aw]) * Q_SCALE
    k = _dot(xb, w_ref[:, c0 + aw:c0 + 2 * aw])
    v = _dot(xb, w_ref[:, c0 + 2 * aw:c0 + 3 * aw])
    g_attn = _dot(xb, w_ref[:, c0 + 3 * aw:c0 + 4 * aw])
    sg_ref[...] = (g_attn * _sigmoid(g_attn)).astype(_BF16)
    tm = nb * rows
    for h in range(F32_SUBLANES):
        for val, stage_ref in ((k, kstage_ref), (v, vstage_ref)):
            src = val[:, h * V_DIM:(h + 1) * V_DIM] if h < n_heads else jnp.zeros((tm, V_DIM), _F32)
            stage_ref[pl.ds(h, tm, stride=F32_SUBLANES), :] = src
    for b in range(nb):
        tile = slice(b * rows * F32_SUBLANES, (b + 1) * rows * F32_SUBLANES)
        krow_ref[b] = kstage_ref[tile, :].reshape(rows, F32_SUBLANES, V_DIM)[:, 0:n_heads, :]
        vrow_ref[b] = vstage_ref[tile, :].reshape(rows, F32_SUBLANES, V_DIM)[:, 0:n_heads, :]
    for b in range(nb):
        for h in range(n_heads):
            rs, cs = slice(b * rows, (b + 1) * rows), slice(h * V_DIM, (h + 1) * V_DIM)
            k_ref[b, h] = k[rs, cs].astype(_BF16)
            if transposed:
                q_ref[b, h] = q[rs, cs].T.astype(_BF16)
                v_ref[b, h, 0:V_DIM, :] = v[rs, cs].T.astype(_BF16)
                ones_row = lax.broadcasted_iota(jnp.int32, (BF16_ROW_TILE, rows), 0) == 0
                v_ref[b, h, V_DIM:VT_ROWS, :] = ones_row.astype(_BF16)
            else:
                q_ref[b, h] = q[rs, cs].astype(_BF16)
                v_ref[b, h] = v[rs, cs].astype(_BF16)


def _in_proj(x2d, hist, w_in_b, pw_bd, pool_b, pool_scale, *, n_seq, seq_len, nb, rows, n_valid,
             transposed):
    d_model = x2d.shape[1]
    pw = pw_bd.shape[0]
    aw = (w_in_b.shape[1] - 2 * pw) // 4
    n_heads = aw // V_DIM
    assert n_seq % nb == 0 and seq_len % rows == 0 and rows >= HIST_ROWS and rows % 8 == 0
    assert nb == 1 or rows == seq_len
    assert n_heads <= F32_SUBLANES
    n_tiles = seq_len // rows
    tm = nb * rows
    grid = (n_seq // nb, n_tiles)
    row_map = lambda s, i: (s * n_tiles + i, 0)
    const2 = lambda s, i: (0, 0)
    hm_shape = jax.ShapeDtypeStruct((n_seq, n_heads, seq_len, V_DIM), _BF16)
    hm_spec = pl.BlockSpec((nb, n_heads, rows, V_DIM), lambda s, i: (s, 0, i, 0))
    if transposed:
        q_shape = jax.ShapeDtypeStruct((n_seq, n_heads, V_DIM, seq_len), _BF16)
        q_spec = pl.BlockSpec((nb, n_heads, V_DIM, rows), lambda s, i: (s, 0, 0, i))
        v_shape = jax.ShapeDtypeStruct((n_seq, n_heads, VT_ROWS, seq_len), _BF16)
        v_spec = pl.BlockSpec((nb, n_heads, VT_ROWS, rows), lambda s, i: (s, 0, 0, i))
    else:
        q_shape, q_spec, v_shape, v_spec = hm_shape, hm_spec, hm_shape, hm_spec
    n_rows = n_seq * seq_len
    rows_shape = jax.ShapeDtypeStruct((n_seq, seq_len, n_heads, V_DIM), _F32)
    rows_spec = pl.BlockSpec((nb, rows, n_heads, V_DIM), lambda s, i: (s, i, 0, 0))
    kernel = functools.partial(_proj_kernel, nb=nb, rows=rows, n_valid=n_valid, pool_width=pw,
                               attn_width=aw, n_heads=n_heads, transposed=transposed)
    return pl.pallas_call(
        kernel,
        grid=grid,
        in_specs=[
            pl.BlockSpec((tm, d_model), row_map),
            pl.BlockSpec((nb, HIST_ROWS, pw), lambda s, i: (s, 0, 0)),
            pl.BlockSpec(w_in_b.shape, const2),
            pl.BlockSpec(pw_bd.shape, const2),
            pl.BlockSpec((1, pw), const2),
            pl.BlockSpec((1, pw), const2),
        ],
        out_specs=[
            rows_spec, rows_spec,
            q_spec, hm_spec, v_spec,
            pl.BlockSpec((tm, aw), row_map),
            pl.BlockSpec((tm, pw), row_map),
            pl.BlockSpec((nb, HIST_ROWS, pw), lambda s, i: (s, 0, 0)),
        ],
        out_shape=[
            rows_shape, rows_shape,
            q_shape, hm_shape, v_shape,
            jax.ShapeDtypeStruct((n_rows, aw), _BF16),
            jax.ShapeDtypeStruct((n_rows, pw), _BF16),
            jax.ShapeDtypeStruct((n_seq, HIST_ROWS, pw), _F32),
        ],
        scratch_shapes=[pltpu.VMEM((nb, HIST_ROWS, pw), _F32),
                        pltpu.VMEM((tm * F32_SUBLANES, V_DIM), _F32),
                        pltpu.VMEM((tm * F32_SUBLANES, V_DIM), _F32)],
        compiler_params=pltpu.CompilerParams(
            dimension_semantics=("arbitrary", "arbitrary"),
            vmem_limit_bytes=V7X_VMEM_LIMIT_BYTES),
        name="in_proj_pool",
    )(x2d, hist, w_in_b, pw_bd, pool_b, pool_scale)


def _lambda_full(lam_ref, lam_init):
    lp = lam_ref[...]
    a1 = jnp.sum(lp[0:1] * lp[1:2], axis=1, keepdims=True)
    a2 = jnp.sum(lp[2:3] * lp[3:4], axis=1, keepdims=True)
    return jnp.exp(a1) - jnp.exp(a2) + lam_init


def _split_maps(q):
    lane = lax.broadcasted_iota(jnp.int32, q.shape, 1)
    zero = jnp.zeros_like(q)
    return jnp.where(lane < HEAD_DIM, q, zero), jnp.where(lane >= HEAD_DIM, q, zero)


def _finish_heads(o1, l1, o2, l2, lam, lam_init, g, sg):
    o = o1 / l1 - lam * (o2 / l2)
    o = o * lax.rsqrt(jnp.mean(o * o, axis=-1, keepdims=True) + SUBLN_EPS)
    o = o * g * (1.0 - lam_init)
    return (sg.astype(_F32) * o).astype(_BF16)


def _sublane_all(op, x):
    for shift in (4, 2, 1):
        x = op(x, pltpu.roll(x, shift=shift, axis=0))
    return x


def _flash_kernel(qT_ref, k_ref, vT_ref, sg_ref, lam_ref, g_ref, o_ref,
                  qz_ref, s0_ref, s1_ref, mx0_ref, mx1_ref, m_ref, acc_ref,
                  *, tq, tk, n_q, lam_init):
    qi = pl.program_id(2)
    s_bufs, mx_bufs = (s0_ref, s1_ref), (mx0_ref, mx1_ref)
    half = tq // 2
    full, upper = (0, tq), (half, half)

    def split_maps(q_tile):
        qT = qT_ref[0, 0, :, pl.ds(pl.multiple_of(q_tile * tq, tq), tq)]
        row = lax.broadcasted_iota(jnp.int32, qT.shape, 0)
        zero = jnp.zeros_like(qT)
        return jnp.where(row < HEAD_DIM, qT, zero), jnp.where(row >= HEAD_DIM, qT, zero)

    def scores(j, slot, masked, q_next, c, lo):
        start = pl.multiple_of(j * tk, tk)
        k = k_ref[0, 0, pl.ds(start, tk), :]
        q_c = qz_ref[c, :, lo:lo + half] if q_next is None else q_next[c][:, lo:lo + half]
        sT = _dot(k, q_c)
        if masked:
            k_pos = start + lax.broadcasted_iota(jnp.int32, (tk, half), 0)
            q_pos = qi * tq + lo + lax.broadcasted_iota(jnp.int32, (tk, half), 1)
            sT = jnp.where(k_pos <= (q_pos | (CHUNK - 1)), sT, -jnp.inf)
        s_bufs[slot][c, :, lo:lo + half] = sT
        mx_bufs[slot][c, :, lo:lo + half] = jnp.max(sT.reshape(tk // 8, 8, half), axis=0)

    def softmax_pv(j, slot, c, lo):
        vT = vT_ref[0, 0, :, pl.ds(pl.multiple_of(j * tk, tk), tk)]
        m_prev = m_ref[c, :, lo:lo + half]
        m_new = jnp.maximum(m_prev, _sublane_all(jnp.maximum, mx_bufs[slot][c, :, lo:lo + half]))
        s3 = s_bufs[slot][c, :, lo:lo + half].reshape(tk // 8, 8, half)
        p = jnp.exp2(s3 - m_new[None]).reshape(tk, half).astype(_BF16)
        pv = _dot(vT, p).reshape(VT_ROWS // 8, 8, half)
        alpha = jnp.exp2(m_prev - m_new)
        acc_ref[c, :, :, lo:lo + half] = alpha[None] * acc_ref[c, :, :, lo:lo + half] + pv
        m_ref[c, :, lo:lo + half] = m_new

    def block(sc, pv):
        units = lambda lanes: [(c, lanes[0] + i * half) for c in range(2) for i in range(lanes[1] // half)]
        sc_units = units(sc[3]) if sc else []
        pv_units = units(pv[2]) if pv else []
        for idx in range(max(len(sc_units), len(pv_units))):
            if idx < len(sc_units):
                scores(sc[0], sc[1], sc[2], sc[4], *sc_units[idx])
            if idx < len(pv_units):
                softmax_pv(pv[0], pv[1], *pv_units[idx])

    qz = split_maps(qi)
    qz_ref[0] = qz[0]
    qz_ref[1] = qz[1]
    m_ref[...] = jnp.full(m_ref.shape, -jnp.inf, _F32)
    acc_ref[...] = jnp.zeros(acc_ref.shape, _F32)
    n_tiles = 2 * qi + 2

    @pl.when(qi == 0)
    def _():
        block((0, 0, True, full, None), None)

    def pair(p, carry):
        block((2 * p + 1, 1, False, full, None), (2 * p, 0, full))
        block((2 * p + 2, 0, False, full, None), (2 * p + 1, 1, full))
        return carry

    lax.fori_loop(0, qi - 1, pair, 0)

    @pl.when(qi > 0)
    def _():
        block((n_tiles - 3, 1, False, full, None), (n_tiles - 4, 0, full))
        block((n_tiles - 2, 0, True, full, None), (n_tiles - 3, 1, full))

    block((n_tiles - 1, 1, True, upper, None), (n_tiles - 2, 0, full))

    @pl.when(qi < n_q - 1)
    def _():
        block((0, 0, False, full, split_maps(qi + 1)), (n_tiles - 1, 1, upper))

    @pl.when(qi == n_q - 1)
    def _():
        block(None, (n_tiles - 1, 1, upper))

    lam = _lambda_full(lam_ref, lam_init).reshape(1, 1, 1)
    n_v = V_DIM // 8
    l1 = _sublane_all(jnp.add, acc_ref[0, n_v])
    l2 = _sublane_all(jnp.add, acc_ref[1, n_v])
    oT = acc_ref[0, 0:n_v] / l1[None] - lam * (acc_ref[1, 0:n_v] / l2[None])
    ms = _sublane_all(jnp.add, jnp.sum(oT * oT, axis=0)) * (1.0 / V_DIM)
    oT = oT * lax.rsqrt(ms + SUBLN_EPS)[None]
    oT = oT.reshape(V_DIM, tq) * g_ref[...] * (1.0 - lam_init)
    o_ref[0] = (sg_ref[0].astype(_F32) * oT.T).astype(_BF16)


def _flash_attention(qT_hm, k_hm, vT_hm, sg, lam_p, subln_g_col, *, tq, lam_init):
    n_seq, n_heads, seq_len, _ = k_hm.shape
    tk = tq // 2
    assert seq_len % tq == 0 and tk % CHUNK == 0 and tk % 128 == 0
    row_spec = pl.BlockSpec((1, tq, V_DIM), lambda b, h, i: (b, i, h))
    return pl.pallas_call(
        functools.partial(_flash_kernel, tq=tq, tk=tk, n_q=seq_len // tq, lam_init=lam_init),
        grid=(n_seq, n_heads, seq_len // tq),
        in_specs=[
            pl.BlockSpec((1, 1, V_DIM, seq_len), lambda b, h, i: (b, h, 0, 0)),
            pl.BlockSpec((1, 1, seq_len, V_DIM), lambda b, h, i: (b, h, 0, 0)),
            pl.BlockSpec((1, 1, VT_ROWS, seq_len), lambda b, h, i: (b, h, 0, 0)),
            row_spec,
            pl.BlockSpec(lam_p.shape, lambda b, h, i: (0, 0)),
            pl.BlockSpec(subln_g_col.shape, lambda b, h, i: (0, 0)),
        ],
        out_specs=row_spec,
        out_shape=jax.ShapeDtypeStruct(sg.shape, _BF16),
        scratch_shapes=[pltpu.VMEM((2, V_DIM, tq), _BF16),
                        pltpu.VMEM((2, tk, tq), _F32), pltpu.VMEM((2, tk, tq), _F32),
                        pltpu.VMEM((2, 8, tq), _F32), pltpu.VMEM((2, 8, tq), _F32),
                        pltpu.VMEM((2, 8, tq), _F32),
                        pltpu.VMEM((2, VT_ROWS // 8, 8, tq), _F32)],
        compiler_params=pltpu.CompilerParams(
            dimension_semantics=("arbitrary", "arbitrary", "arbitrary"),
            vmem_limit_bytes=V7X_VMEM_LIMIT_BYTES),
        name="flash_diff_attn",
    )(qT_hm, k_hm, vT_hm, sg, lam_p, subln_g_col)


def _decode_attn_kernel(q_ref, k_ref, v_ref, ck_ref, cv_ref, sg_ref, lam_ref, g_ref, o_ref,
                        *, n_heads, vis_past, vis_new, lam_init):
    lam = _lambda_full(lam_ref, lam_init)
    for h in range(n_heads):
        cs = slice(h * V_DIM, (h + 1) * V_DIM)
        qs = _split_maps(q_ref[0, h])
        k_new, v_new = k_ref[0, h], v_ref[0, h]
        k_past = ck_ref[0, :, h, :].astype(_BF16)
        v_past = cv_ref[0, :, h, :].astype(_BF16)
        outs = []
        for c in range(2):
            s_past = _dot_nt(qs[c], k_past)
            s_new = _dot_nt(qs[c], k_new)
            if vis_past is not None:
                s_past = jnp.where(vis_past, s_past, -jnp.inf)
            if vis_new is not None:
                s_new = jnp.where(vis_new, s_new, -jnp.inf)
            m = jnp.maximum(jnp.max(s_past, axis=1, keepdims=True),
                            jnp.max(s_new, axis=1, keepdims=True))
            p_past = jnp.exp2(s_past - m)
            p_new = jnp.exp2(s_new - m)
            l = jnp.sum(p_past, axis=1, keepdims=True) + jnp.sum(p_new, axis=1, keepdims=True)
            o = _dot(p_past.astype(_BF16), v_past) + _dot(p_new.astype(_BF16), v_new)
            outs += [o, l]
        o_ref[0, :, cs] = _finish_heads(*outs, lam, lam_init, g_ref[...], sg_ref[0, :, cs])


def _static_visibility(q_pos, k_pos):
    vis = (k_pos[None, :] // CHUNK) <= (q_pos[:, None] // CHUNK)
    return None if vis.all() else jnp.asarray(vis)


def _decode_attention(q_hm, k_hm, v_hm, ck, cv, sg, lam_p, subln_g, *, lam_init):
    n_seq, n_heads, t_new, _ = q_hm.shape
    past = ck.shape[1]
    q_pos = past + np.arange(t_new)
    vis_past = _static_visibility(q_pos, np.arange(past))
    vis_new = _static_visibility(q_pos, q_pos)
    new_spec = pl.BlockSpec((1, n_heads, t_new, V_DIM), lambda b: (b, 0, 0, 0))
    cache_spec = pl.BlockSpec((1, past, n_heads, V_DIM), lambda b: (b, 0, 0, 0))
    row_spec = pl.BlockSpec((1, t_new, n_heads * V_DIM), lambda b: (b, 0, 0))
    return pl.pallas_call(
        functools.partial(_decode_attn_kernel, n_heads=n_heads, vis_past=vis_past, vis_new=vis_new,
                          lam_init=lam_init),
        grid=(n_seq,),
        in_specs=[new_spec, new_spec, new_spec, cache_spec, cache_spec, row_spec,
                  pl.BlockSpec(lam_p.shape, lambda b: (0, 0)),
                  pl.BlockSpec(subln_g.shape, lambda b: (0, 0))],
        out_specs=row_spec,
        out_shape=jax.ShapeDtypeStruct(sg.shape, _BF16),
        compiler_params=pltpu.CompilerParams(
            dimension_semantics=("arbitrary",), vmem_limit_bytes=V7X_VMEM_LIMIT_BYTES),
        name="decode_diff_attn",
    )(q_hm, k_hm, v_hm, ck, cv, sg, lam_p, subln_g)


def _out_kernel(x_ref, mp_ref, ma_ref, w_ref, g_ref, b_ref, y_ref, *, pool_width, alpha):
    out = _dot(mp_ref[...], w_ref[0:pool_width, :]) + _dot(ma_ref[...], w_ref[pool_width:, :])
    z = alpha * x_ref[...] + out
    mu = jnp.mean(z, axis=-1, keepdims=True)
    zc = z - mu
    var = jnp.mean(zc * zc, axis=-1, keepdims=True)
    y_ref[...] = zc * lax.rsqrt(var + LN_EPS) * g_ref[...] + b_ref[...]


def _out_proj(x2d, mp, ma, w_out_b, ln_g, ln_b, *, tm, alpha):
    n_rows, d_model = x2d.shape
    pw, aw = mp.shape[1], ma.shape[1]
    assert n_rows % tm == 0
    row = lambda i: (i, 0)
    const = lambda i: (0, 0)
    return pl.pallas_call(
        functools.partial(_out_kernel, pool_width=pw, alpha=alpha),
        grid=(n_rows // tm,),
        in_specs=[pl.BlockSpec((tm, d_model), row), pl.BlockSpec((tm, pw), row),
                  pl.BlockSpec((tm, aw), row), pl.BlockSpec(w_out_b.shape, const),
                  pl.BlockSpec((1, d_model), const), pl.BlockSpec((1, d_model), const)],
        out_specs=pl.BlockSpec((tm, d_model), row),
        out_shape=jax.ShapeDtypeStruct((n_rows, d_model), _F32),
        compiler_params=pltpu.CompilerParams(
            dimension_semantics=("arbitrary",), vmem_limit_bytes=V7X_VMEM_LIMIT_BYTES),
        name="out_proj_ln",
    )(x2d, mp, ma, w_out_b, ln_g, ln_b)


def _block_diag(pool_w):
    g, c, d = pool_w.shape
    out = jnp.zeros((g * c, g * d), pool_w.dtype)
    for i in range(g):
        out = out.at[i * c:(i + 1) * c, i * d:(i + 1) * d].set(pool_w[i])
    return out


def _layer(x, hist, n_valid, k_past, v_past, lam_init, alpha, params, *, rows, tq, tm_out):
    (w_in, pool_w, pool_b, pool_scale, lq1, lk1, lq2, lk2, subln_g, w_out, ln_g, ln_b) = params
    n_seq, seq_len, d_model = x.shape
    pw = pool_scale.shape[0]
    aw = w_out.shape[0] - pw
    n_heads = aw // V_DIM
    x2d = x.reshape(n_seq * seq_len, d_model)
    w_in_b, w_out_b = w_in.astype(_BF16), w_out.astype(_BF16)
    pw_bd = _block_diag(pool_w).astype(_BF16)
    lam_p = jnp.stack([lq1, lk1, lq2, lk2]).astype(_F32)
    prompt = k_past is None
    nb = 1 if prompt else n_seq
    krow, vrow, q_hm, k_hm, v_hm, sg, mp, pst = _in_proj(
        x2d, hist, w_in_b, pw_bd, pool_b.reshape(1, pw).astype(_F32),
        pool_scale.reshape(1, pw).astype(_F32),
        n_seq=n_seq, seq_len=seq_len, nb=nb, rows=rows, n_valid=n_valid, transposed=prompt)
    sg3 = sg.reshape(n_seq, seq_len, aw)
    if prompt:
        ma = _flash_attention(q_hm, k_hm, v_hm, sg3, lam_p, subln_g.reshape(V_DIM, 1).astype(_F32),
                              tq=tq, lam_init=lam_init)
    else:
        ma = _decode_attention(q_hm, k_hm, v_hm, k_past, v_past, sg3, lam_p,
                               subln_g.reshape(1, V_DIM).astype(_F32), lam_init=lam_init)
    y = _out_proj(x2d, mp, ma.reshape(n_seq * seq_len, aw), w_out_b,
                  ln_g.reshape(1, d_model).astype(_F32), ln_b.reshape(1, d_model).astype(_F32),
                  tm=tm_out, alpha=alpha)
    return y.reshape(n_seq, seq_len, d_model), krow, vrow, pst


def kernel(x_prompt, x_sample, cache_k, cache_v, state_pool, w_in, pool_w, pool_b, pool_scale,
           lambda_q1, lambda_k1, lambda_q2, lambda_k2, subln_g, w_out, ln_g, ln_b):
    depth = w_in.shape[0]
    pw = pool_scale.shape[1]
    pool_hist = max(POOL_WINDOWS) - 1
    alpha = (2.0 * depth) ** 0.25
    n_valid_sample = min(cache_k.shape[2], pool_hist)
    yp, ys = x_prompt, x_sample
    outs = [[] for _ in range(6)]
    for l in range(depth):
        lam_init = _lambda_init(l)
        params = (w_in[l], pool_w[l], pool_b[l], pool_scale[l], lambda_q1[l], lambda_k1[l],
                  lambda_q2[l], lambda_k2[l], subln_g[l], w_out[l], ln_g[l], ln_b[l])
        zero_hist = jnp.zeros((yp.shape[0], HIST_ROWS, pw), _F32)
        yp, kp, vp, pp = _layer(yp, zero_hist, 0, None, None, lam_init, alpha, params,
                                rows=512, tq=1024, tm_out=512)
        hist_s = jnp.pad(state_pool[l].astype(_F32), ((0, 0), (HIST_ROWS - pool_hist, 0), (0, 0)))
        ys, kn, vn, pn = _layer(ys, hist_s, n_valid_sample, cache_k[l], cache_v[l], lam_init, alpha,
                                params, rows=ys.shape[1], tq=None,
                                tm_out=ys.shape[0] * ys.shape[1])
        for lst, val in zip(outs, (kp, vp, pp[:, HIST_ROWS - pool_hist:], kn, vn,
                                   pn[:, HIST_ROWS - pool_hist:])):
            lst.append(val)
    return (yp, ys) + tuple(o[0][None] if depth == 1 else jnp.stack(o) for o in outs)
```

```python
import functools
import math

import numpy as np
import jax
import jax.numpy as jnp
from jax import lax
from jax.experimental import pallas as pl
from jax.experimental.pallas import tpu as pltpu

CHUNK = 64
POOL_WINDOWS = (2, 4, 8, 16)
POOL_GROUPS = len(POOL_WINDOWS)
HEAD_DIM = 64
V_DIM = 2 * HEAD_DIM
ATTN_SCALE = HEAD_DIM ** -0.5
Q_SCALE = ATTN_SCALE * math.log2(math.e)
LN_EPS = 1e-5
SUBLN_EPS = 1e-5

BF16_ROW_TILE = 16
VT_ROWS = V_DIM + BF16_ROW_TILE

HIST_ROWS = 16
V7X_VMEM_LIMIT_BYTES = 56 * 1024 * 1024

_BF16 = jnp.bfloat16
_F32 = jnp.float32


def _lambda_init(layer_idx):
    return 0.8 - 0.6 * math.exp(-0.3 * layer_idx)


def _sigmoid(x):
    return 1.0 / (1.0 + jnp.exp(-x))


def _dot(a, b):
    return jnp.dot(a, b, preferred_element_type=_F32)


def _dot_nt(a, b):
    return lax.dot_general(a, b, (((1,), (1,)), ((), ())), preferred_element_type=_F32)


def _proj_kernel(x_ref, hist_ref, w_ref, pw_ref, pb_ref, ps_ref,
                 krow_ref, vrow_ref, q_ref, k_ref, v_ref, sg_ref, mp_ref, pst_ref,
                 carry_ref, *, nb, rows, n_valid, pool_width, attn_width, n_heads, transposed):
    i = pl.program_id(1)
    pw, aw = pool_width, attn_width
    group_dim = pw // POOL_GROUPS

    @pl.when(i == 0)
    def _():
        carry_ref[...] = hist_ref[...]

    xb = x_ref[...].astype(_BF16)
    u = _dot(xb, w_ref[:, 0:pw])
    g_pool = _dot(xb, w_ref[:, pw:2 * pw])

    ext_rows = HIST_ROWS + rows
    lane = lax.broadcasted_iota(jnp.int32, (ext_rows, pw), 1)
    row = lax.broadcasted_iota(jnp.int32, (ext_rows, pw), 0)
    t_pos = i * rows + row - HIST_ROWS
    win = jnp.full((ext_rows, pw), POOL_WINDOWS[-1], jnp.int32)
    for g in range(POOL_GROUPS - 2, -1, -1):
        win = jnp.where(lane < (g + 1) * group_dim, POOL_WINDOWS[g], win)
    cnt = jnp.maximum(jnp.minimum(win, n_valid + t_pos + 1), 1).astype(_F32)
    pooled_parts = []
    for b in range(nb):
        u_b = u[b * rows:(b + 1) * rows]
        ext = jnp.concatenate([carry_ref[b], u_b], axis=0)
        sums = {1: ext}
        w = 1
        while w < POOL_WINDOWS[-1]:
            sums[2 * w] = sums[w] + pltpu.roll(sums[w], shift=w, axis=0)
            w *= 2
        sel = sums[POOL_WINDOWS[-1]]
        for g in range(POOL_GROUPS - 2, -1, -1):
            sel = jnp.where(lane < (g + 1) * group_dim, sums[POOL_WINDOWS[g]], sel)
        pooled_parts.append((sel / cnt - ext)[HIST_ROWS:])
        carry_ref[b] = u_b[rows - HIST_ROWS:]
        pst_ref[b] = u_b[rows - HIST_ROWS:]
    pooled = pooled_parts[0] if nb == 1 else jnp.concatenate(pooled_parts, axis=0)
    mixed = (_dot(pooled.astype(_BF16), pw_ref[...]) + pb_ref[...]) * ps_ref[...]
    mp_ref[...] = (g_pool * _sigmoid(g_pool) * mixed).astype(_BF16)

    c0 = 2 * pw
    q = _dot(xb, w_ref[:, c0:c0 + aw]) * Q_SCALE
    k = _dot(xb, w_ref[:, c0 + aw:c0 + 2 * aw])
    v = _dot(xb, w_ref[:, c0 + 2 * aw:c0 + 3 * aw])
    g_attn = _dot(xb, w_ref[:, c0 + 3 * aw:c0 + 4 * aw])
    sg_ref[...] = (g_attn * _sigmoid(g_attn)).astype(_BF16)
    for b in range(nb):
        for h in range(n_heads):
            rs, cs = slice(b * rows, (b + 1) * rows), slice(h * V_DIM, (h + 1) * V_DIM)
            krow_ref[b, h] = k[rs, cs]
            vrow_ref[b, h] = v[rs, cs]
            k_ref[b, h] = k[rs, cs].astype(_BF16)
            if transposed:
                q_ref[b, h] = q[rs, cs].T.astype(_BF16)
                v_ref[b, h, 0:V_DIM, :] = v[rs, cs].T.astype(_BF16)
                ones_row = lax.broadcasted_iota(jnp.int32, (BF16_ROW_TILE, rows), 0) == 0
                v_ref[b, h, V_DIM:VT_ROWS, :] = ones_row.astype(_BF16)
            else:
                q_ref[b, h] = q[rs, cs].astype(_BF16)
                v_ref[b, h] = v[rs, cs].astype(_BF16)


def _in_proj(x2d, hist, w_in_b, pw_bd, pool_b, pool_scale, *, n_seq, seq_len, nb, rows, n_valid,
             transposed):
    d_model = x2d.shape[1]
    pw = pw_bd.shape[0]
    aw = (w_in_b.shape[1] - 2 * pw) // 4
    n_heads = aw // V_DIM
    assert n_seq % nb == 0 and seq_len % rows == 0 and rows >= HIST_ROWS and rows % 8 == 0
    assert nb == 1 or rows == seq_len
    n_tiles = seq_len // rows
    tm = nb * rows
    grid = (n_seq // nb, n_tiles)
    row_map = lambda s, i: (s * n_tiles + i, 0)
    const2 = lambda s, i: (0, 0)
    hm_shape = jax.ShapeDtypeStruct((n_seq, n_heads, seq_len, V_DIM), _BF16)
    hm_spec = pl.BlockSpec((nb, n_heads, rows, V_DIM), lambda s, i: (s, 0, i, 0))
    if transposed:
        q_shape = jax.ShapeDtypeStruct((n_seq, n_heads, V_DIM, seq_len), _BF16)
        q_spec = pl.BlockSpec((nb, n_heads, V_DIM, rows), lambda s, i: (s, 0, 0, i))
        v_shape = jax.ShapeDtypeStruct((n_seq, n_heads, VT_ROWS, seq_len), _BF16)
        v_spec = pl.BlockSpec((nb, n_heads, VT_ROWS, rows), lambda s, i: (s, 0, 0, i))
    else:
        q_shape, q_spec, v_shape, v_spec = hm_shape, hm_spec, hm_shape, hm_spec
    n_rows = n_seq * seq_len
    rows_shape = jax.ShapeDtypeStruct((n_seq, n_heads, seq_len, V_DIM), _F32)
    rows_spec = hm_spec
    kernel = functools.partial(_proj_kernel, nb=nb, rows=rows, n_valid=n_valid, pool_width=pw,
                               attn_width=aw, n_heads=n_heads, transposed=transposed)
    return pl.pallas_call(
        kernel,
        grid=grid,
        in_specs=[
            pl.BlockSpec((tm, d_model), row_map),
            pl.BlockSpec((nb, HIST_ROWS, pw), lambda s, i: (s, 0, 0)),
            pl.BlockSpec(w_in_b.shape, const2),
            pl.BlockSpec(pw_bd.shape, const2),
            pl.BlockSpec((1, pw), const2),
            pl.BlockSpec((1, pw), const2),
        ],
        out_specs=[
            rows_spec, rows_spec,
            q_spec, hm_spec, v_spec,
            pl.BlockSpec((tm, aw), row_map),
            pl.BlockSpec((tm, pw), row_map),
            pl.BlockSpec((nb, HIST_ROWS, pw), lambda s, i: (s, 0, 0)),
        ],
        out_shape=[
            rows_shape, rows_shape,
            q_shape, hm_shape, v_shape,
            jax.ShapeDtypeStruct((n_rows, aw), _BF16),
            jax.ShapeDtypeStruct((n_rows, pw), _BF16),
            jax.ShapeDtypeStruct((n_seq, HIST_ROWS, pw), _F32),
        ],
        scratch_shapes=[pltpu.VMEM((nb, HIST_ROWS, pw), _F32)],
        compiler_params=pltpu.CompilerParams(
            dimension_semantics=("arbitrary", "arbitrary"),
            vmem_limit_bytes=V7X_VMEM_LIMIT_BYTES),
        name="in_proj_pool",
    )(x2d, hist, w_in_b, pw_bd, pool_b, pool_scale)


def _lambda_full(lam_ref, lam_init):
    lp = lam_ref[...]
    a1 = jnp.sum(lp[0:1] * lp[1:2], axis=1, keepdims=True)
    a2 = jnp.sum(lp[2:3] * lp[3:4], axis=1, keepdims=True)
    return jnp.exp(a1) - jnp.exp(a2) + lam_init


def _split_maps(q):
    lane = lax.broadcasted_iota(jnp.int32, q.shape, 1)
    zero = jnp.zeros_like(q)
    return jnp.where(lane < HEAD_DIM, q, zero), jnp.where(lane >= HEAD_DIM, q, zero)


def _finish_heads(o1, l1, o2, l2, lam, lam_init, g, sg):
    o = o1 / l1 - lam * (o2 / l2)
    o = o * lax.rsqrt(jnp.mean(o * o, axis=-1, keepdims=True) + SUBLN_EPS)
    o = o * g * (1.0 - lam_init)
    return (sg.astype(_F32) * o).astype(_BF16)


def _sublane_all(op, x):
    for shift in (4, 2, 1):
        x = op(x, pltpu.roll(x, shift=shift, axis=0))
    return x


def _flash_kernel(qT_ref, k_ref, vT_ref, sg_ref, lam_ref, g_ref, o_ref,
                  qz_ref, s0_ref, s1_ref, mx0_ref, mx1_ref, m_ref, acc_ref,
                  *, tq, tk, n_q, lam_init):
    qi = pl.program_id(2)
    s_bufs, mx_bufs = (s0_ref, s1_ref), (mx0_ref, mx1_ref)
    half = tq // 2
    full, upper = (0, tq), (half, half)

    def split_maps(q_tile):
        qT = qT_ref[0, 0, :, pl.ds(pl.multiple_of(q_tile * tq, tq), tq)]
        row = lax.broadcasted_iota(jnp.int32, qT.shape, 0)
        zero = jnp.zeros_like(qT)
        return jnp.where(row < HEAD_DIM, qT, zero), jnp.where(row >= HEAD_DIM, qT, zero)

    def scores(j, slot, masked, q_next, c, lo):
        start = pl.multiple_of(j * tk, tk)
        k = k_ref[0, 0, pl.ds(start, tk), :]
        q_c = qz_ref[c, :, lo:lo + half] if q_next is None else q_next[c][:, lo:lo + half]
        sT = _dot(k, q_c)
        if masked:
            k_pos = start + lax.broadcasted_iota(jnp.int32, (tk, half), 0)
            q_pos = qi * tq + lo + lax.broadcasted_iota(jnp.int32, (tk, half), 1)
            sT = jnp.where(k_pos <= (q_pos | (CHUNK - 1)), sT, -jnp.inf)
        s_bufs[slot][c, :, lo:lo + half] = sT
        mx_bufs[slot][c, :, lo:lo + half] = jnp.max(sT.reshape(tk // 8, 8, half), axis=0)

    def softmax_pv(j, slot, c, lo):
        vT = vT_ref[0, 0, :, pl.ds(pl.multiple_of(j * tk, tk), tk)]
        m_prev = m_ref[c, :, lo:lo + half]
        m_new = jnp.maximum(m_prev, _sublane_all(jnp.maximum, mx_bufs[slot][c, :, lo:lo + half]))
        s3 = s_bufs[slot][c, :, lo:lo + half].reshape(tk // 8, 8, half)
        p = jnp.exp2(s3 - m_new[None]).reshape(tk, half).astype(_BF16)
        pv = _dot(vT, p).reshape(VT_ROWS // 8, 8, half)
        alpha = jnp.exp2(m_prev - m_new)
        acc_ref[c, :, :, lo:lo + half] = alpha[None] * acc_ref[c, :, :, lo:lo + half] + pv
        m_ref[c, :, lo:lo + half] = m_new

    def block(sc, pv):
        units = lambda lanes: [(c, lanes[0] + i * half) for c in range(2) for i in range(lanes[1] // half)]
        sc_units = units(sc[3]) if sc else []
        pv_units = units(pv[2]) if pv else []
        for idx in range(max(len(sc_units), len(pv_units))):
            if idx < len(sc_units):
                scores(sc[0], sc[1], sc[2], sc[4], *sc_units[idx])
            if idx < len(pv_units):
                softmax_pv(pv[0], pv[1], *pv_units[idx])

    qz = split_maps(qi)
    qz_ref[0] = qz[0]
    qz_ref[1] = qz[1]
    m_ref[...] = jnp.full(m_ref.shape, -jnp.inf, _F32)
    acc_ref[...] = jnp.zeros(acc_ref.shape, _F32)
    n_tiles = 2 * qi + 2

    @pl.when(qi == 0)
    def _():
        block((0, 0, True, full, None), None)

    def pair(p, carry):
        block((2 * p + 1, 1, False, full, None), (2 * p, 0, full))
        block((2 * p + 2, 0, False, full, None), (2 * p + 1, 1, full))
        return carry

    lax.fori_loop(0, qi - 1, pair, 0)

    @pl.when(qi > 0)
    def _():
        block((n_tiles - 3, 1, False, full, None), (n_tiles - 4, 0, full))
        block((n_tiles - 2, 0, True, full, None), (n_tiles - 3, 1, full))

    block((n_tiles - 1, 1, True, upper, None), (n_tiles - 2, 0, full))

    @pl.when(qi < n_q - 1)
    def _():
        block((0, 0, False, full, split_maps(qi + 1)), (n_tiles - 1, 1, upper))

    @pl.when(qi == n_q - 1)
    def _():
        block(None, (n_tiles - 1, 1, upper))

    lam = _lambda_full(lam_ref, lam_init).reshape(1, 1, 1)
    n_v = V_DIM // 8
    l1 = _sublane_all(jnp.add, acc_ref[0, n_v])
    l2 = _sublane_all(jnp.add, acc_ref[1, n_v])
    oT = acc_ref[0, 0:n_v] / l1[None] - lam * (acc_ref[1, 0:n_v] / l2[None])
    ms = _sublane_all(jnp.add, jnp.sum(oT * oT, axis=0)) * (1.0 / V_DIM)
    oT = oT * lax.rsqrt(ms + SUBLN_EPS)[None]
    oT = oT.reshape(V_DIM, tq) * g_ref[...] * (1.0 - lam_init)
    o_ref[0] = (sg_ref[0].astype(_F32) * oT.T).astype(_BF16)


def _flash_attention(qT_hm, k_hm, vT_hm, sg, lam_p, subln_g_col, *, tq, lam_init):
    n_seq, n_heads, seq_len, _ = k_hm.shape
    tk = tq // 2
    assert seq_len % tq == 0 and tk % CHUNK == 0 and tk % 128 == 0
    row_spec = pl.BlockSpec((1, tq, V_DIM), lambda b, h, i: (b, i, h))
    return pl.pallas_call(
        functools.partial(_flash_kernel, tq=tq, tk=tk, n_q=seq_len // tq, lam_init=lam_init),
        grid=(n_seq, n_heads, seq_len // tq),
        in_specs=[
            pl.BlockSpec((1, 1, V_DIM, seq_len), lambda b, h, i: (b, h, 0, 0)),
            pl.BlockSpec((1, 1, seq_len, V_DIM), lambda b, h, i: (b, h, 0, 0)),
            pl.BlockSpec((1, 1, VT_ROWS, seq_len), lambda b, h, i: (b, h, 0, 0)),
            row_spec,
            pl.BlockSpec(lam_p.shape, lambda b, h, i: (0, 0)),
            pl.BlockSpec(subln_g_col.shape, lambda b, h, i: (0, 0)),
        ],
        out_specs=row_spec,
        out_shape=jax.ShapeDtypeStruct(sg.shape, _BF16),
        scratch_shapes=[pltpu.VMEM((2, V_DIM, tq), _BF16),
                        pltpu.VMEM((2, tk, tq), _F32), pltpu.VMEM((2, tk, tq), _F32),
                        pltpu.VMEM((2, 8, tq), _F32), pltpu.VMEM((2, 8, tq), _F32),
                        pltpu.VMEM((2, 8, tq), _F32),
                        pltpu.VMEM((2, VT_ROWS // 8, 8, tq), _F32)],
        compiler_params=pltpu.CompilerParams(
            dimension_semantics=("arbitrary", "arbitrary", "arbitrary"),
            vmem_limit_bytes=V7X_VMEM_LIMIT_BYTES),
        name="flash_diff_attn",
    )(qT_hm, k_hm, vT_hm, sg, lam_p, subln_g_col)


def _decode_attn_kernel(q_ref, k_ref, v_ref, ck_ref, cv_ref, sg_ref, lam_ref, g_ref, o_ref,
                        *, n_heads, vis_past, vis_new, lam_init):
    lam = _lambda_full(lam_ref, lam_init)
    for h in range(n_heads):
        cs = slice(h * V_DIM, (h + 1) * V_DIM)
        qs = _split_maps(q_ref[0, h])
        k_new, v_new = k_ref[0, h], v_ref[0, h]
        k_past = ck_ref[0, h].astype(_BF16)
        v_past = cv_ref[0, h].astype(_BF16)
        outs = []
        for c in range(2):
            s_past = _dot_nt(qs[c], k_past)
            s_new = _dot_nt(qs[c], k_new)
            if vis_past is not None:
                s_past = jnp.where(vis_past, s_past, -jnp.inf)
            if vis_new is not None:
                s_new = jnp.where(vis_new, s_new, -jnp.inf)
            m = jnp.maximum(jnp.max(s_past, axis=1, keepdims=True),
                            jnp.max(s_new, axis=1, keepdims=True))
            p_past = jnp.exp2(s_past - m)
            p_new = jnp.exp2(s_new - m)
            l = jnp.sum(p_past, axis=1, keepdims=True) + jnp.sum(p_new, axis=1, keepdims=True)
            o = _dot(p_past.astype(_BF16), v_past) + _dot(p_new.astype(_BF16), v_new)
            outs += [o, l]
        o_ref[0, :, cs] = _finish_heads(*outs, lam, lam_init, g_ref[...], sg_ref[0, :, cs])


def _static_visibility(q_pos, k_pos):
    vis = (k_pos[None, :] // CHUNK) <= (q_pos[:, None] // CHUNK)
    return None if vis.all() else jnp.asarray(vis)


def _decode_attention(q_hm, k_hm, v_hm, ck, cv, sg, lam_p, subln_g, *, lam_init):
    n_seq, n_heads, t_new, _ = q_hm.shape
    past = ck.shape[2]
    q_pos = past + np.arange(t_new)
    vis_past = _static_visibility(q_pos, np.arange(past))
    vis_new = _static_visibility(q_pos, q_pos)
    new_spec = pl.BlockSpec((1, n_heads, t_new, V_DIM), lambda b: (b, 0, 0, 0))
    cache_spec = pl.BlockSpec((1, n_heads, past, V_DIM), lambda b: (b, 0, 0, 0))
    row_spec = pl.BlockSpec((1, t_new, n_heads * V_DIM), lambda b: (b, 0, 0))
    return pl.pallas_call(
        functools.partial(_decode_attn_kernel, n_heads=n_heads, vis_past=vis_past, vis_new=vis_new,
                          lam_init=lam_init),
        grid=(n_seq,),
        in_specs=[new_spec, new_spec, new_spec, cache_spec, cache_spec, row_spec,
                  pl.BlockSpec(lam_p.shape, lambda b: (0, 0)),
                  pl.BlockSpec(subln_g.shape, lambda b: (0, 0))],
        out_specs=row_spec,
        out_shape=jax.ShapeDtypeStruct(sg.shape, _BF16),
        compiler_params=pltpu.CompilerParams(
            dimension_semantics=("arbitrary",), vmem_limit_bytes=V7X_VMEM_LIMIT_BYTES),
        name="decode_diff_attn",
    )(q_hm, k_hm, v_hm, ck, cv, sg, lam_p, subln_g)


def _out_kernel(x_ref, mp_ref, ma_ref, w_ref, g_ref, b_ref, y_ref, *, pool_width, alpha):
    out = _dot(mp_ref[...], w_ref[0:pool_width, :]) + _dot(ma_ref[...], w_ref[pool_width:, :])
    z = alpha * x_ref[...] + out
    mu = jnp.mean(z, axis=-1, keepdims=True)
    zc = z - mu
    var = jnp.mean(zc * zc, axis=-1, keepdims=True)
    y_ref[...] = zc * lax.rsqrt(var + LN_EPS) * g_ref[...] + b_ref[...]


def _out_proj(x2d, mp, ma, w_out_b, ln_g, ln_b, *, tm, alpha):
    n_rows, d_model = x2d.shape
    pw, aw = mp.shape[1], ma.shape[1]
    assert n_rows % tm == 0
    row = lambda i: (i, 0)
    const = lambda i: (0, 0)
    return pl.pallas_call(
        functools.partial(_out_kernel, pool_width=pw, alpha=alpha),
        grid=(n_rows // tm,),
        in_specs=[pl.BlockSpec((tm, d_model), row), pl.BlockSpec((tm, pw), row),
                  pl.BlockSpec((tm, aw), row), pl.BlockSpec(w_out_b.shape, const),
                  pl.BlockSpec((1, d_model), const), pl.BlockSpec((1, d_model), const)],
        out_specs=pl.BlockSpec((tm, d_model), row),
        out_shape=jax.ShapeDtypeStruct((n_rows, d_model), _F32),
        compiler_params=pltpu.CompilerParams(
            dimension_semantics=("arbitrary",), vmem_limit_bytes=V7X_VMEM_LIMIT_BYTES),
        name="out_proj_ln",
    )(x2d, mp, ma, w_out_b, ln_g, ln_b)


def _block_diag(pool_w):
    g, c, d = pool_w.shape
    out = jnp.zeros((g * c, g * d), pool_w.dtype)
    for i in range(g):
        out = out.at[i * c:(i + 1) * c, i * d:(i + 1) * d].set(pool_w[i])
    return out


def _swap_seq_head(rows):
    return jnp.transpose(rows, (0, 2, 1, 3))


def _layer(x, hist, n_valid, k_past, v_past, lam_init, alpha, params, *, rows, tq, tm_out):
    (w_in, pool_w, pool_b, pool_scale, lq1, lk1, lq2, lk2, subln_g, w_out, ln_g, ln_b) = params
    n_seq, seq_len, d_model = x.shape
    pw = pool_scale.shape[0]
    aw = w_out.shape[0] - pw
    n_heads = aw // V_DIM
    x2d = x.reshape(n_seq * seq_len, d_model)
    w_in_b, w_out_b = w_in.astype(_BF16), w_out.astype(_BF16)
    pw_bd = _block_diag(pool_w).astype(_BF16)
    lam_p = jnp.stack([lq1, lk1, lq2, lk2]).astype(_F32)
    prompt = k_past is None
    nb = 1 if prompt else n_seq
    krow, vrow, q_hm, k_hm, v_hm, sg, mp, pst = _in_proj(
        x2d, hist, w_in_b, pw_bd, pool_b.reshape(1, pw).astype(_F32),
        pool_scale.reshape(1, pw).astype(_F32),
        n_seq=n_seq, seq_len=seq_len, nb=nb, rows=rows, n_valid=n_valid, transposed=prompt)
    sg3 = sg.reshape(n_seq, seq_len, aw)
    if prompt:
        ma = _flash_attention(q_hm, k_hm, v_hm, sg3, lam_p, subln_g.reshape(V_DIM, 1).astype(_F32),
                              tq=tq, lam_init=lam_init)
    else:
        ma = _decode_attention(q_hm, k_hm, v_hm, _swap_seq_head(k_past), _swap_seq_head(v_past), sg3,
                               lam_p, subln_g.reshape(1, V_DIM).astype(_F32), lam_init=lam_init)
    y = _out_proj(x2d, mp, ma.reshape(n_seq * seq_len, aw), w_out_b,
                  ln_g.reshape(1, d_model).astype(_F32), ln_b.reshape(1, d_model).astype(_F32),
                  tm=tm_out, alpha=alpha)
    return y.reshape(n_seq, seq_len, d_model), _swap_seq_head(krow), _swap_seq_head(vrow), pst


def kernel(x_prompt, x_sample, cache_k, cache_v, state_pool, w_in, pool_w, pool_b, pool_scale,
           lambda_q1, lambda_k1, lambda_q2, lambda_k2, subln_g, w_out, ln_g, ln_b):
    depth = w_in.shape[0]
    pw = pool_scale.shape[1]
    pool_hist = max(POOL_WINDOWS) - 1
    alpha = (2.0 * depth) ** 0.25
    n_valid_sample = min(cache_k.shape[2], pool_hist)
    yp, ys = x_prompt, x_sample
    outs = [[] for _ in range(6)]
    for l in range(depth):
        lam_init = _lambda_init(l)
        params = (w_in[l], pool_w[l], pool_b[l], pool_scale[l], lambda_q1[l], lambda_k1[l],
                  lambda_q2[l], lambda_k2[l], subln_g[l], w_out[l], ln_g[l], ln_b[l])
        zero_hist = jnp.zeros((yp.shape[0], HIST_ROWS, pw), _F32)
        yp, kp, vp, pp = _layer(yp, zero_hist, 0, None, None, lam_init, alpha, params,
                                rows=512, tq=1024, tm_out=512)
        hist_s = jnp.pad(state_pool[l].astype(_F32), ((0, 0), (HIST_ROWS - pool_hist, 0), (0, 0)))
        ys, kn, vn, pn = _layer(ys, hist_s, n_valid_sample, cache_k[l], cache_v[l], lam_init, alpha,
                                params, rows=ys.shape[1], tq=None,
                                tm_out=ys.shape[0] * ys.shape[1])
        for lst, val in zip(outs, (kp, vp, pp[:, HIST_ROWS - pool_hist:], kn, vn,
                                   pn[:, HIST_ROWS - pool_hist:])):
            lst.append(val)
    return (yp, ys) + tuple(o[0][None] if depth == 1 else jnp.stack(o) for o in outs)
```

```python
import functools
import math

import numpy as np
import jax
import jax.numpy as jnp
from jax import lax
from jax.experimental import pallas as pl
from jax.experimental.pallas import tpu as pltpu

CHUNK = 64
POOL_WINDOWS = (2, 4, 8, 16)
POOL_GROUPS = len(POOL_WINDOWS)
HEAD_DIM = 64
V_DIM = 2 * HEAD_DIM
ATTN_SCALE = HEAD_DIM ** -0.5
Q_SCALE = ATTN_SCALE * math.log2(math.e)
LN_EPS = 1e-5
SUBLN_EPS = 1e-5

BF16_ROW_TILE = 16
VT_ROWS = V_DIM + BF16_ROW_TILE

HIST_ROWS = 16
V7X_VMEM_LIMIT_BYTES = 56 * 1024 * 1024
OUT_ROW_CHUNK = 256

_BF16 = jnp.bfloat16
_F32 = jnp.float32


def _lambda_init(layer_idx):
    return 0.8 - 0.6 * math.exp(-0.3 * layer_idx)


def _sigmoid(x):
    return 1.0 / (1.0 + jnp.exp(-x))


def _dot(a, b):
    return jnp.dot(a, b, preferred_element_type=_F32)


def _dot_nt(a, b):
    return lax.dot_general(a, b, (((1,), (1,)), ((), ())), preferred_element_type=_F32)


def _proj_kernel(x_ref, hist_ref, w_ref, pw_ref, pb_ref, ps_ref,
                 krow_ref, vrow_ref, q_ref, k_ref, v_ref, sg_ref, mp_ref, pst_ref,
                 carry_ref, *, nb, rows, n_valid, pool_width, attn_width, n_heads, transposed):
    i = pl.program_id(1)
    pw, aw = pool_width, attn_width
    group_dim = pw // POOL_GROUPS

    @pl.when(i == 0)
    def _():
        carry_ref[...] = hist_ref[...]

    xb = x_ref[...].astype(_BF16)
    u = _dot(xb, w_ref[:, 0:pw])
    g_pool = _dot(xb, w_ref[:, pw:2 * pw])

    ext_rows = HIST_ROWS + rows
    lane = lax.broadcasted_iota(jnp.int32, (ext_rows, pw), 1)
    row = lax.broadcasted_iota(jnp.int32, (ext_rows, pw), 0)
    t_pos = i * rows + row - HIST_ROWS
    win = jnp.full((ext_rows, pw), POOL_WINDOWS[-1], jnp.int32)
    for g in range(POOL_GROUPS - 2, -1, -1):
        win = jnp.where(lane < (g + 1) * group_dim, POOL_WINDOWS[g], win)
    cnt = jnp.maximum(jnp.minimum(win, n_valid + t_pos + 1), 1).astype(_F32)
    pooled_parts = []
    for b in range(nb):
        u_b = u[b * rows:(b + 1) * rows]
        ext = jnp.concatenate([carry_ref[b], u_b], axis=0)
        sums = {1: ext}
        w = 1
        while w < POOL_WINDOWS[-1]:
            sums[2 * w] = sums[w] + pltpu.roll(sums[w], shift=w, axis=0)
            w *= 2
        sel = sums[POOL_WINDOWS[-1]]
        for g in range(POOL_GROUPS - 2, -1, -1):
            sel = jnp.where(lane < (g + 1) * group_dim, sums[POOL_WINDOWS[g]], sel)
        pooled_parts.append((sel / cnt - ext)[HIST_ROWS:])
        carry_ref[b] = u_b[rows - HIST_ROWS:]
        pst_ref[b] = u_b[rows - HIST_ROWS:]
    pooled = pooled_parts[0] if nb == 1 else jnp.concatenate(pooled_parts, axis=0)
    mixed = (_dot(pooled.astype(_BF16), pw_ref[...]) + pb_ref[...]) * ps_ref[...]
    mp_ref[...] = (g_pool * _sigmoid(g_pool) * mixed).astype(_BF16)

    c0 = 2 * pw
    q = _dot(xb, w_ref[:, c0:c0 + aw]) * Q_SCALE
    k = _dot(xb, w_ref[:, c0 + aw:c0 + 2 * aw])
    v = _dot(xb, w_ref[:, c0 + 2 * aw:c0 + 3 * aw])
    g_attn = _dot(xb, w_ref[:, c0 + 3 * aw:c0 + 4 * aw])
    sg_ref[...] = (g_attn * _sigmoid(g_attn)).astype(_BF16)
    for b in range(nb):
        for h in range(n_heads):
            rs, cs = slice(b * rows, (b + 1) * rows), slice(h * V_DIM, (h + 1) * V_DIM)
            krow_ref[b, h] = k[rs, cs]
            vrow_ref[b, h] = v[rs, cs]
            k_ref[b, h] = k[rs, cs].astype(_BF16)
            if transposed:
                q_ref[b, h] = q[rs, cs].T.astype(_BF16)
                v_ref[b, h, 0:V_DIM, :] = v[rs, cs].T.astype(_BF16)
                ones_row = lax.broadcasted_iota(jnp.int32, (BF16_ROW_TILE, rows), 0) == 0
                v_ref[b, h, V_DIM:VT_ROWS, :] = ones_row.astype(_BF16)
            else:
                q_ref[b, h] = q[rs, cs].astype(_BF16)
                v_ref[b, h] = v[rs, cs].astype(_BF16)


def _in_proj(x2d, hist, w_in_b, pw_bd, pool_b, pool_scale, *, n_seq, seq_len, nb, rows, n_valid,
             transposed):
    d_model = x2d.shape[1]
    pw = pw_bd.shape[0]
    aw = (w_in_b.shape[1] - 2 * pw) // 4
    n_heads = aw // V_DIM
    assert n_seq % nb == 0 and seq_len % rows == 0 and rows >= HIST_ROWS and rows % 8 == 0
    assert nb == 1 or rows == seq_len
    n_tiles = seq_len // rows
    tm = nb * rows
    grid = (n_seq // nb, n_tiles)
    row_map = lambda s, i: (s * n_tiles + i, 0)
    const2 = lambda s, i: (0, 0)
    hm_shape = jax.ShapeDtypeStruct((n_seq, n_heads, seq_len, V_DIM), _BF16)
    hm_spec = pl.BlockSpec((nb, n_heads, rows, V_DIM), lambda s, i: (s, 0, i, 0))
    if transposed:
        q_shape = jax.ShapeDtypeStruct((n_seq, n_heads, V_DIM, seq_len), _BF16)
        q_spec = pl.BlockSpec((nb, n_heads, V_DIM, rows), lambda s, i: (s, 0, 0, i))
        v_shape = jax.ShapeDtypeStruct((n_seq, n_heads, VT_ROWS, seq_len), _BF16)
        v_spec = pl.BlockSpec((nb, n_heads, VT_ROWS, rows), lambda s, i: (s, 0, 0, i))
    else:
        q_shape, q_spec, v_shape, v_spec = hm_shape, hm_spec, hm_shape, hm_spec
    n_rows = n_seq * seq_len
    rows_shape = jax.ShapeDtypeStruct((n_seq, n_heads, seq_len, V_DIM), _F32)
    rows_spec = hm_spec
    kernel = functools.partial(_proj_kernel, nb=nb, rows=rows, n_valid=n_valid, pool_width=pw,
                               attn_width=aw, n_heads=n_heads, transposed=transposed)
    return pl.pallas_call(
        kernel,
        grid=grid,
        in_specs=[
            pl.BlockSpec((tm, d_model), row_map),
            pl.BlockSpec((nb, HIST_ROWS, pw), lambda s, i: (s, 0, 0)),
            pl.BlockSpec(w_in_b.shape, const2),
            pl.BlockSpec(pw_bd.shape, const2),
            pl.BlockSpec((1, pw), const2),
            pl.BlockSpec((1, pw), const2),
        ],
        out_specs=[
            rows_spec, rows_spec,
            q_spec, hm_spec, v_spec,
            pl.BlockSpec((tm, aw), row_map),
            pl.BlockSpec((tm, pw), row_map),
            pl.BlockSpec((nb, HIST_ROWS, pw), lambda s, i: (s, 0, 0)),
        ],
        out_shape=[
            rows_shape, rows_shape,
            q_shape, hm_shape, v_shape,
            jax.ShapeDtypeStruct((n_rows, aw), _BF16),
            jax.ShapeDtypeStruct((n_rows, pw), _BF16),
            jax.ShapeDtypeStruct((n_seq, HIST_ROWS, pw), _F32),
        ],
        scratch_shapes=[pltpu.VMEM((nb, HIST_ROWS, pw), _F32)],
        compiler_params=pltpu.CompilerParams(
            dimension_semantics=("arbitrary", "arbitrary"),
            vmem_limit_bytes=V7X_VMEM_LIMIT_BYTES),
        name="in_proj_pool",
    )(x2d, hist, w_in_b, pw_bd, pool_b, pool_scale)


def _lambda_full(lam_ref, lam_init):
    lp = lam_ref[...]
    a1 = jnp.sum(lp[0:1] * lp[1:2], axis=1, keepdims=True)
    a2 = jnp.sum(lp[2:3] * lp[3:4], axis=1, keepdims=True)
    return jnp.exp(a1) - jnp.exp(a2) + lam_init


def _split_maps(q):
    lane = lax.broadcasted_iota(jnp.int32, q.shape, 1)
    zero = jnp.zeros_like(q)
    return jnp.where(lane < HEAD_DIM, q, zero), jnp.where(lane >= HEAD_DIM, q, zero)


def _finish_heads(o1, l1, o2, l2, lam, lam_init, g, sg):
    o = o1 / l1 - lam * (o2 / l2)
    o = o * lax.rsqrt(jnp.mean(o * o, axis=-1, keepdims=True) + SUBLN_EPS)
    o = o * g * (1.0 - lam_init)
    return (sg.astype(_F32) * o).astype(_BF16)


def _sublane_all(op, x):
    for shift in (4, 2, 1):
        x = op(x, pltpu.roll(x, shift=shift, axis=0))
    return x


def _flash_kernel(qT_ref, k_ref, vT_ref, sg_ref, lam_ref, g_ref, o_ref,
                  qz_ref, s0_ref, s1_ref, mx0_ref, mx1_ref, m_ref, acc_ref,
                  *, tq, tk, n_q, lam_init):
    qi = pl.program_id(2)
    s_bufs, mx_bufs = (s0_ref, s1_ref), (mx0_ref, mx1_ref)
    half = tq // 2
    full, upper = (0, tq), (half, half)

    def split_maps(q_tile):
        qT = qT_ref[0, 0, :, pl.ds(pl.multiple_of(q_tile * tq, tq), tq)]
        row = lax.broadcasted_iota(jnp.int32, qT.shape, 0)
        zero = jnp.zeros_like(qT)
        return jnp.where(row < HEAD_DIM, qT, zero), jnp.where(row >= HEAD_DIM, qT, zero)

    def scores(j, slot, masked, q_next, c, lo):
        start = pl.multiple_of(j * tk, tk)
        k = k_ref[0, 0, pl.ds(start, tk), :]
        q_c = qz_ref[c, :, lo:lo + half] if q_next is None else q_next[c][:, lo:lo + half]
        sT = _dot(k, q_c)
        if masked:
            k_pos = start + lax.broadcasted_iota(jnp.int32, (tk, half), 0)
            q_pos = qi * tq + lo + lax.broadcasted_iota(jnp.int32, (tk, half), 1)
            sT = jnp.where(k_pos <= (q_pos | (CHUNK - 1)), sT, -jnp.inf)
        s_bufs[slot][c, :, lo:lo + half] = sT
        mx_bufs[slot][c, :, lo:lo + half] = jnp.max(sT.reshape(tk // 8, 8, half), axis=0)

    def softmax_pv(j, slot, c, lo):
        vT = vT_ref[0, 0, :, pl.ds(pl.multiple_of(j * tk, tk), tk)]
        m_prev = m_ref[c, :, lo:lo + half]
        m_new = jnp.maximum(m_prev, _sublane_all(jnp.maximum, mx_bufs[slot][c, :, lo:lo + half]))
        s3 = s_bufs[slot][c, :, lo:lo + half].reshape(tk // 8, 8, half)
        p = jnp.exp2(s3 - m_new[None]).reshape(tk, half).astype(_BF16)
        pv = _dot(vT, p).reshape(VT_ROWS // 8, 8, half)
        alpha = jnp.exp2(m_prev - m_new)
        acc_ref[c, :, :, lo:lo + half] = alpha[None] * acc_ref[c, :, :, lo:lo + half] + pv
        m_ref[c, :, lo:lo + half] = m_new

    def block(sc, pv):
        units = lambda lanes: [(c, lanes[0] + i * half) for c in range(2) for i in range(lanes[1] // half)]
        sc_units = units(sc[3]) if sc else []
        pv_units = units(pv[2]) if pv else []
        for idx in range(max(len(sc_units), len(pv_units))):
            if idx < len(sc_units):
                scores(sc[0], sc[1], sc[2], sc[4], *sc_units[idx])
            if idx < len(pv_units):
                softmax_pv(pv[0], pv[1], *pv_units[idx])

    qz = split_maps(qi)
    qz_ref[0] = qz[0]
    qz_ref[1] = qz[1]
    m_ref[...] = jnp.full(m_ref.shape, -jnp.inf, _F32)
    acc_ref[...] = jnp.zeros(acc_ref.shape, _F32)
    n_tiles = 2 * qi + 2

    @pl.when(qi == 0)
    def _():
        block((0, 0, True, full, None), None)

    def pair(p):
        block((2 * p + 1, 1, False, full, None), (2 * p, 0, full))
        block((2 * p + 2, 0, False, full, None), (2 * p + 1, 1, full))

    def two_pairs(i, carry):
        pair(2 * i)
        pair(2 * i + 1)
        return carry

    lax.fori_loop(0, lax.shift_right_logical(jnp.maximum(qi - 1, 0), 1), two_pairs, 0)

    @pl.when(jnp.logical_and(qi >= 2, (qi & 1) == 0))
    def _():
        pair(qi - 2)

    @pl.when(qi > 0)
    def _():
        block((n_tiles - 3, 1, False, full, None), (n_tiles - 4, 0, full))
        block((n_tiles - 2, 0, True, full, None), (n_tiles - 3, 1, full))

    block((n_tiles - 1, 1, True, upper, None), (n_tiles - 2, 0, full))

    @pl.when(qi < n_q - 1)
    def _():
        block((0, 0, False, full, split_maps(qi + 1)), (n_tiles - 1, 1, upper))

    @pl.when(qi == n_q - 1)
    def _():
        block(None, (n_tiles - 1, 1, upper))

    lam = _lambda_full(lam_ref, lam_init).reshape(1, 1, 1)
    n_v = V_DIM // 8
    l1 = _sublane_all(jnp.add, acc_ref[0, n_v])
    l2 = _sublane_all(jnp.add, acc_ref[1, n_v])
    oT = acc_ref[0, 0:n_v] / l1[None] - lam * (acc_ref[1, 0:n_v] / l2[None])
    ms = _sublane_all(jnp.add, jnp.sum(oT * oT, axis=0)) * (1.0 / V_DIM)
    oT = oT * lax.rsqrt(ms + SUBLN_EPS)[None]
    oT = oT.reshape(V_DIM, tq) * g_ref[...] * (1.0 - lam_init)
    o_ref[0] = (sg_ref[0].astype(_F32) * oT.T).astype(_BF16)


def _flash_attention(qT_hm, k_hm, vT_hm, sg, lam_p, subln_g_col, *, tq, lam_init):
    n_seq, n_heads, seq_len, _ = k_hm.shape
    tk = tq // 2
    assert seq_len % tq == 0 and tk % CHUNK == 0 and tk % 128 == 0
    row_spec = pl.BlockSpec((1, tq, V_DIM), lambda b, h, i: (b, i, h))
    return pl.pallas_call(
        functools.partial(_flash_kernel, tq=tq, tk=tk, n_q=seq_len // tq, lam_init=lam_init),
        grid=(n_seq, n_heads, seq_len // tq),
        in_specs=[
            pl.BlockSpec((1, 1, V_DIM, seq_len), lambda b, h, i: (b, h, 0, 0)),
            pl.BlockSpec((1, 1, seq_len, V_DIM), lambda b, h, i: (b, h, 0, 0)),
            pl.BlockSpec((1, 1, VT_ROWS, seq_len), lambda b, h, i: (b, h, 0, 0)),
            row_spec,
            pl.BlockSpec(lam_p.shape, lambda b, h, i: (0, 0)),
            pl.BlockSpec(subln_g_col.shape, lambda b, h, i: (0, 0)),
        ],
        out_specs=row_spec,
        out_shape=jax.ShapeDtypeStruct(sg.shape, _BF16),
        scratch_shapes=[pltpu.VMEM((2, V_DIM, tq), _BF16),
                        pltpu.VMEM((2, tk, tq), _F32), pltpu.VMEM((2, tk, tq), _F32),
                        pltpu.VMEM((2, 8, tq), _F32), pltpu.VMEM((2, 8, tq), _F32),
                        pltpu.VMEM((2, 8, tq), _F32),
                        pltpu.VMEM((2, VT_ROWS // 8, 8, tq), _F32)],
        compiler_params=pltpu.CompilerParams(
            dimension_semantics=("arbitrary", "arbitrary", "arbitrary"),
            vmem_limit_bytes=V7X_VMEM_LIMIT_BYTES),
        name="flash_diff_attn",
    )(qT_hm, k_hm, vT_hm, sg, lam_p, subln_g_col)


def _decode_attn_kernel(q_ref, k_ref, v_ref, ck_ref, cv_ref, sg_ref, lam_ref, g_ref, o_ref,
                        *, n_heads, vis_past, vis_new, lam_init):
    lam = _lambda_full(lam_ref, lam_init)
    for h in range(n_heads):
        cs = slice(h * V_DIM, (h + 1) * V_DIM)
        qs = _split_maps(q_ref[0, h])
        k_new, v_new = k_ref[0, h], v_ref[0, h]
        k_past = ck_ref[0, h].astype(_BF16)
        v_past = cv_ref[0, h].astype(_BF16)
        outs = []
        for c in range(2):
            s_past = _dot_nt(qs[c], k_past)
            s_new = _dot_nt(qs[c], k_new)
            if vis_past is not None:
                s_past = jnp.where(vis_past, s_past, -jnp.inf)
            if vis_new is not None:
                s_new = jnp.where(vis_new, s_new, -jnp.inf)
            m = jnp.maximum(jnp.max(s_past, axis=1, keepdims=True),
                            jnp.max(s_new, axis=1, keepdims=True))
            p_past = jnp.exp2(s_past - m)
            p_new = jnp.exp2(s_new - m)
            l = jnp.sum(p_past, axis=1, keepdims=True) + jnp.sum(p_new, axis=1, keepdims=True)
            o = _dot(p_past.astype(_BF16), v_past) + _dot(p_new.astype(_BF16), v_new)
            outs += [o, l]
        o_ref[0, :, cs] = _finish_heads(*outs, lam, lam_init, g_ref[...], sg_ref[0, :, cs])


def _static_visibility(q_pos, k_pos):
    vis = (k_pos[None, :] // CHUNK) <= (q_pos[:, None] // CHUNK)
    return None if vis.all() else jnp.asarray(vis)


def _decode_attention(q_hm, k_hm, v_hm, ck, cv, sg, lam_p, subln_g, *, lam_init):
    n_seq, n_heads, t_new, _ = q_hm.shape
    past = ck.shape[2]
    q_pos = past + np.arange(t_new)
    vis_past = _static_visibility(q_pos, np.arange(past))
    vis_new = _static_visibility(q_pos, q_pos)
    new_spec = pl.BlockSpec((1, n_heads, t_new, V_DIM), lambda b: (b, 0, 0, 0))
    cache_spec = pl.BlockSpec((1, n_heads, past, V_DIM), lambda b: (b, 0, 0, 0))
    row_spec = pl.BlockSpec((1, t_new, n_heads * V_DIM), lambda b: (b, 0, 0))
    return pl.pallas_call(
        functools.partial(_decode_attn_kernel, n_heads=n_heads, vis_past=vis_past, vis_new=vis_new,
                          lam_init=lam_init),
        grid=(n_seq,),
        in_specs=[new_spec, new_spec, new_spec, cache_spec, cache_spec, row_spec,
                  pl.BlockSpec(lam_p.shape, lambda b: (0, 0)),
                  pl.BlockSpec(subln_g.shape, lambda b: (0, 0))],
        out_specs=row_spec,
        out_shape=jax.ShapeDtypeStruct(sg.shape, _BF16),
        compiler_params=pltpu.CompilerParams(
            dimension_semantics=("arbitrary",), vmem_limit_bytes=V7X_VMEM_LIMIT_BYTES),
        name="decode_diff_attn",
    )(q_hm, k_hm, v_hm, ck, cv, sg, lam_p, subln_g)


def _out_kernel(x_ref, mp_ref, ma_ref, w_ref, g_ref, b_ref, y_ref, *, pool_width, alpha, chunk):
    for r0 in range(0, x_ref.shape[0], chunk):
        rs = slice(r0, r0 + chunk)
        out = _dot(mp_ref[rs, :], w_ref[0:pool_width, :]) + _dot(ma_ref[rs, :], w_ref[pool_width:, :])
        z = alpha * x_ref[rs, :] + out
        mu = jnp.mean(z, axis=-1, keepdims=True)
        zc = z - mu
        var = jnp.mean(zc * zc, axis=-1, keepdims=True)
        y_ref[rs, :] = zc * lax.rsqrt(var + LN_EPS) * g_ref[...] + b_ref[...]


def _out_proj(x2d, mp, ma, w_out_b, ln_g, ln_b, *, tm, alpha):
    n_rows, d_model = x2d.shape
    pw, aw = mp.shape[1], ma.shape[1]
    assert n_rows % tm == 0
    row = lambda i: (i, 0)
    const = lambda i: (0, 0)
    return pl.pallas_call(
        functools.partial(_out_kernel, pool_width=pw, alpha=alpha, chunk=min(tm, OUT_ROW_CHUNK)),
        grid=(n_rows // tm,),
        in_specs=[pl.BlockSpec((tm, d_model), row), pl.BlockSpec((tm, pw), row),
                  pl.BlockSpec((tm, aw), row), pl.BlockSpec(w_out_b.shape, const),
                  pl.BlockSpec((1, d_model), const), pl.BlockSpec((1, d_model), const)],
        out_specs=pl.BlockSpec((tm, d_model), row),
        out_shape=jax.ShapeDtypeStruct((n_rows, d_model), _F32),
        compiler_params=pltpu.CompilerParams(
            dimension_semantics=("arbitrary",), vmem_limit_bytes=V7X_VMEM_LIMIT_BYTES),
        name="out_proj_ln",
    )(x2d, mp, ma, w_out_b, ln_g, ln_b)


def _block_diag(pool_w):
    g, c, d = pool_w.shape
    out = jnp.zeros((g * c, g * d), pool_w.dtype)
    for i in range(g):
        out = out.at[i * c:(i + 1) * c, i * d:(i + 1) * d].set(pool_w[i])
    return out


def _swap_seq_head(rows):
    return jnp.transpose(rows, (0, 2, 1, 3))


def _layer(x, hist, n_valid, k_past, v_past, lam_init, alpha, params, *, rows, tq, tm_out):
    (w_in, pool_w, pool_b, pool_scale, lq1, lk1, lq2, lk2, subln_g, w_out, ln_g, ln_b) = params
    n_seq, seq_len, d_model = x.shape
    pw = pool_scale.shape[0]
    aw = w_out.shape[0] - pw
    n_heads = aw // V_DIM
    x2d = x.reshape(n_seq * seq_len, d_model)
    w_in_b, w_out_b = w_in.astype(_BF16), w_out.astype(_BF16)
    pw_bd = _block_diag(pool_w).astype(_BF16)
    lam_p = jnp.stack([lq1, lk1, lq2, lk2]).astype(_F32)
    prompt = k_past is None
    nb = 1 if prompt else n_seq
    krow, vrow, q_hm, k_hm, v_hm, sg, mp, pst = _in_proj(
        x2d, hist, w_in_b, pw_bd, pool_b.reshape(1, pw).astype(_F32),
        pool_scale.reshape(1, pw).astype(_F32),
        n_seq=n_seq, seq_len=seq_len, nb=nb, rows=rows, n_valid=n_valid, transposed=prompt)
    sg3 = sg.reshape(n_seq, seq_len, aw)
    if prompt:
        ma = _flash_attention(q_hm, k_hm, v_hm, sg3, lam_p, subln_g.reshape(V_DIM, 1).astype(_F32),
                              tq=tq, lam_init=lam_init)
    else:
        ma = _decode_attention(q_hm, k_hm, v_hm, _swap_seq_head(k_past), _swap_seq_head(v_past), sg3,
                               lam_p, subln_g.reshape(1, V_DIM).astype(_F32), lam_init=lam_init)
    y = _out_proj(x2d, mp, ma.reshape(n_seq * seq_len, aw), w_out_b,
                  ln_g.reshape(1, d_model).astype(_F32), ln_b.reshape(1, d_model).astype(_F32),
                  tm=tm_out, alpha=alpha)
    return y.reshape(n_seq, seq_len, d_model), _swap_seq_head(krow), _swap_seq_head(vrow), pst


def kernel(x_prompt, x_sample, cache_k, cache_v, state_pool, w_in, pool_w, pool_b, pool_scale,
           lambda_q1, lambda_k1, lambda_q2, lambda_k2, subln_g, w_out, ln_g, ln_b):
    depth = w_in.shape[0]
    pw = pool_scale.shape[1]
    pool_hist = max(POOL_WINDOWS) - 1
    alpha = (2.0 * depth) ** 0.25
    n_valid_sample = min(cache_k.shape[2], pool_hist)
    yp, ys = x_prompt, x_sample
    outs = [[] for _ in range(6)]
    for l in range(depth):
        lam_init = _lambda_init(l)
        params = (w_in[l], pool_w[l], pool_b[l], pool_scale[l], lambda_q1[l], lambda_k1[l],
                  lambda_q2[l], lambda_k2[l], subln_g[l], w_out[l], ln_g[l], ln_b[l])
        zero_hist = jnp.zeros((yp.shape[0], HIST_ROWS, pw), _F32)
        yp, kp, vp, pp = _layer(yp, zero_hist, 0, None, None, lam_init, alpha, params,
                                rows=512, tq=1024, tm_out=1024)
        hist_s = jnp.pad(state_pool[l].astype(_F32), ((0, 0), (HIST_ROWS - pool_hist, 0), (0, 0)))
        ys, kn, vn, pn = _layer(ys, hist_s, n_valid_sample, cache_k[l], cache_v[l], lam_init, alpha,
                                params, rows=ys.shape[1], tq=None,
                                tm_out=ys.shape[0] * ys.shape[1])
        for lst, val in zip(outs, (kp, vp, pp[:, HIST_ROWS - pool_hist:], kn, vn,
                                   pn[:, HIST_ROWS - pool_hist:])):
            lst.append(val)
    return (yp, ys) + tuple(o[0][None] if depth == 1 else jnp.stack(o) for o in outs)
```

```python
import functools
import math

import numpy as np
import jax
import jax.numpy as jnp
from jax import lax
from jax.experimental import pallas as pl
from jax.experimental.pallas import tpu as pltpu

CHUNK = 64
POOL_WINDOWS = (2, 4, 8, 16)
POOL_GROUPS = len(POOL_WINDOWS)
HEAD_DIM = 64
V_DIM = 2 * HEAD_DIM
ATTN_SCALE = HEAD_DIM ** -0.5
Q_SCALE = ATTN_SCALE * math.log2(math.e)
LN_EPS = 1e-5
SUBLN_EPS = 1e-5

BF16_ROW_TILE = 16
VT_ROWS = V_DIM + BF16_ROW_TILE

HIST_ROWS = 16
V7X_VMEM_LIMIT_BYTES = 56 * 1024 * 1024
OUT_ROW_CHUNK = 256

_BF16 = jnp.bfloat16
_F32 = jnp.float32


def _lambda_init(layer_idx):
    return 0.8 - 0.6 * math.exp(-0.3 * layer_idx)


def _sigmoid(x):
    return 1.0 / (1.0 + jnp.exp(-x))


def _dot(a, b):
    return jnp.dot(a, b, preferred_element_type=_F32)


def _dot_nt(a, b):
    return lax.dot_general(a, b, (((1,), (1,)), ((), ())), preferred_element_type=_F32)


def _proj_kernel(x_ref, hist_ref, w_ref, pw_ref, pb_ref, ps_ref,
                 krow_ref, vrow_ref, q_ref, k_ref, v_ref, sg_ref, mp_ref, pst_ref,
                 carry_ref, *, nb, rows, n_valid, pool_width, attn_width, n_heads, transposed):
    i = pl.program_id(1)
    pw, aw = pool_width, attn_width
    group_dim = pw // POOL_GROUPS

    @pl.when(i == 0)
    def _():
        carry_ref[...] = hist_ref[...]

    xb = x_ref[...].astype(_BF16)
    u = _dot(xb, w_ref[:, 0:pw])
    g_pool = _dot(xb, w_ref[:, pw:2 * pw])
    c0 = 2 * pw

    def head_blocks(val, write):
        for b in range(nb):
            for h in range(n_heads):
                write(b, h, val[b * rows:(b + 1) * rows, h * V_DIM:(h + 1) * V_DIM])

    def write_q(b, h, blk):
        q_ref[b, h] = (blk.T if transposed else blk).astype(_BF16)

    def write_k(b, h, blk):
        krow_ref[b, h] = blk
        k_ref[b, h] = blk.astype(_BF16)

    def write_v(b, h, blk):
        vrow_ref[b, h] = blk
        if transposed:
            v_ref[b, h, 0:V_DIM, :] = blk.T.astype(_BF16)
            ones_row = lax.broadcasted_iota(jnp.int32, (BF16_ROW_TILE, rows), 0) == 0
            v_ref[b, h, V_DIM:VT_ROWS, :] = ones_row.astype(_BF16)
        else:
            v_ref[b, h] = blk.astype(_BF16)

    head_blocks(_dot(xb, w_ref[:, c0:c0 + aw]) * Q_SCALE, write_q)

    ext_rows = HIST_ROWS + rows
    lane = lax.broadcasted_iota(jnp.int32, (ext_rows, pw), 1)
    row = lax.broadcasted_iota(jnp.int32, (ext_rows, pw), 0)
    t_pos = i * rows + row - HIST_ROWS
    win = jnp.full((ext_rows, pw), POOL_WINDOWS[-1], jnp.int32)
    for g in range(POOL_GROUPS - 2, -1, -1):
        win = jnp.where(lane < (g + 1) * group_dim, POOL_WINDOWS[g], win)
    cnt = jnp.maximum(jnp.minimum(win, n_valid + t_pos + 1), 1).astype(_F32)
    pooled_parts = []
    for b in range(nb):
        u_b = u[b * rows:(b + 1) * rows]
        ext = jnp.concatenate([carry_ref[b], u_b], axis=0)
        sums = {1: ext}
        w = 1
        while w < POOL_WINDOWS[-1]:
            sums[2 * w] = sums[w] + pltpu.roll(sums[w], shift=w, axis=0)
            w *= 2
        sel = sums[POOL_WINDOWS[-1]]
        for g in range(POOL_GROUPS - 2, -1, -1):
            sel = jnp.where(lane < (g + 1) * group_dim, sums[POOL_WINDOWS[g]], sel)
        pooled_parts.append((sel / cnt - ext)[HIST_ROWS:])
        carry_ref[b] = u_b[rows - HIST_ROWS:]
        pst_ref[b] = u_b[rows - HIST_ROWS:]
    pooled = pooled_parts[0] if nb == 1 else jnp.concatenate(pooled_parts, axis=0)
    mixed = (_dot(pooled.astype(_BF16), pw_ref[...]) + pb_ref[...]) * ps_ref[...]
    mp_ref[...] = (g_pool * _sigmoid(g_pool) * mixed).astype(_BF16)

    head_blocks(_dot(xb, w_ref[:, c0 + aw:c0 + 2 * aw]), write_k)
    head_blocks(_dot(xb, w_ref[:, c0 + 2 * aw:c0 + 3 * aw]), write_v)
    g_attn = _dot(xb, w_ref[:, c0 + 3 * aw:c0 + 4 * aw])
    sg_ref[...] = (g_attn * _sigmoid(g_attn)).astype(_BF16)


def _in_proj(x2d, hist, w_in_b, pw_bd, pool_b, pool_scale, *, n_seq, seq_len, nb, rows, n_valid,
             transposed):
    d_model = x2d.shape[1]
    pw = pw_bd.shape[0]
    aw = (w_in_b.shape[1] - 2 * pw) // 4
    n_heads = aw // V_DIM
    assert n_seq % nb == 0 and seq_len % rows == 0 and rows >= HIST_ROWS and rows % 8 == 0
    assert nb == 1 or rows == seq_len
    n_tiles = seq_len // rows
    tm = nb * rows
    grid = (n_seq // nb, n_tiles)
    row_map = lambda s, i: (s * n_tiles + i, 0)
    const2 = lambda s, i: (0, 0)
    hm_shape = jax.ShapeDtypeStruct((n_seq, n_heads, seq_len, V_DIM), _BF16)
    hm_spec = pl.BlockSpec((nb, n_heads, rows, V_DIM), lambda s, i: (s, 0, i, 0))
    if transposed:
        q_shape = jax.ShapeDtypeStruct((n_seq, n_heads, V_DIM, seq_len), _BF16)
        q_spec = pl.BlockSpec((nb, n_heads, V_DIM, rows), lambda s, i: (s, 0, 0, i))
        v_shape = jax.ShapeDtypeStruct((n_seq, n_heads, VT_ROWS, seq_len), _BF16)
        v_spec = pl.BlockSpec((nb, n_heads, VT_ROWS, rows), lambda s, i: (s, 0, 0, i))
    else:
        q_shape, q_spec, v_shape, v_spec = hm_shape, hm_spec, hm_shape, hm_spec
    n_rows = n_seq * seq_len
    rows_shape = jax.ShapeDtypeStruct((n_seq, n_heads, seq_len, V_DIM), _F32)
    rows_spec = hm_spec
    kernel = functools.partial(_proj_kernel, nb=nb, rows=rows, n_valid=n_valid, pool_width=pw,
                               attn_width=aw, n_heads=n_heads, transposed=transposed)
    return pl.pallas_call(
        kernel,
        grid=grid,
        in_specs=[
            pl.BlockSpec((tm, d_model), row_map),
            pl.BlockSpec((nb, HIST_ROWS, pw), lambda s, i: (s, 0, 0)),
            pl.BlockSpec(w_in_b.shape, const2),
            pl.BlockSpec(pw_bd.shape, const2),
            pl.BlockSpec((1, pw), const2),
            pl.BlockSpec((1, pw), const2),
        ],
        out_specs=[
            rows_spec, rows_spec,
            q_spec, hm_spec, v_spec,
            pl.BlockSpec((tm, aw), row_map),
            pl.BlockSpec((tm, pw), row_map),
            pl.BlockSpec((nb, HIST_ROWS, pw), lambda s, i: (s, 0, 0)),
        ],
        out_shape=[
            rows_shape, rows_shape,
            q_shape, hm_shape, v_shape,
            jax.ShapeDtypeStruct((n_rows, aw), _BF16),
            jax.ShapeDtypeStruct((n_rows, pw), _BF16),
            jax.ShapeDtypeStruct((n_seq, HIST_ROWS, pw), _F32),
        ],
        scratch_shapes=[pltpu.VMEM((nb, HIST_ROWS, pw), _F32)],
        compiler_params=pltpu.CompilerParams(
            dimension_semantics=("arbitrary", "arbitrary"),
            vmem_limit_bytes=V7X_VMEM_LIMIT_BYTES),
        name="in_proj_pool",
    )(x2d, hist, w_in_b, pw_bd, pool_b, pool_scale)


def _lambda_full(lam_ref, lam_init):
    lp = lam_ref[...]
    a1 = jnp.sum(lp[0:1] * lp[1:2], axis=1, keepdims=True)
    a2 = jnp.sum(lp[2:3] * lp[3:4], axis=1, keepdims=True)
    return jnp.exp(a1) - jnp.exp(a2) + lam_init


def _split_maps(q):
    lane = lax.broadcasted_iota(jnp.int32, q.shape, 1)
    zero = jnp.zeros_like(q)
    return jnp.where(lane < HEAD_DIM, q, zero), jnp.where(lane >= HEAD_DIM, q, zero)


def _finish_heads(o1, l1, o2, l2, lam, lam_init, g, sg):
    o = o1 / l1 - lam * (o2 / l2)
    o = o * lax.rsqrt(jnp.mean(o * o, axis=-1, keepdims=True) + SUBLN_EPS)
    o = o * g * (1.0 - lam_init)
    return (sg.astype(_F32) * o).astype(_BF16)


def _sublane_all(op, x):
    for shift in (4, 2, 1):
        x = op(x, pltpu.roll(x, shift=shift, axis=0))
    return x


def _flash_kernel(qT_ref, k_ref, vT_ref, sg_ref, lam_ref, g_ref, o_ref,
                  qz_ref, s0_ref, s1_ref, mx0_ref, mx1_ref, m_ref, acc_ref,
                  *, tq, tk, n_q, lam_init):
    qi = pl.program_id(2)
    s_bufs, mx_bufs = (s0_ref, s1_ref), (mx0_ref, mx1_ref)
    half = tq // 2
    full, upper = (0, tq), (half, half)

    def split_maps(q_tile):
        qT = qT_ref[0, 0, :, pl.ds(pl.multiple_of(q_tile * tq, tq), tq)]
        row = lax.broadcasted_iota(jnp.int32, qT.shape, 0)
        zero = jnp.zeros_like(qT)
        return jnp.where(row < HEAD_DIM, qT, zero), jnp.where(row >= HEAD_DIM, qT, zero)

    def scores(j, slot, masked, q_next, c, lo):
        start = pl.multiple_of(j * tk, tk)
        k = k_ref[0, 0, pl.ds(start, tk), :]
        q_c = qz_ref[c, :, lo:lo + half] if q_next is None else q_next[c][:, lo:lo + half]
        sT = _dot(k, q_c)
        if masked:
            k_pos = start + lax.broadcasted_iota(jnp.int32, (tk, half), 0)
            q_pos = qi * tq + lo + lax.broadcasted_iota(jnp.int32, (tk, half), 1)
            sT = jnp.where(k_pos <= (q_pos | (CHUNK - 1)), sT, -jnp.inf)
        s_bufs[slot][c, :, lo:lo + half] = sT
        mx_bufs[slot][c, :, lo:lo + half] = jnp.max(sT.reshape(tk // 8, 8, half), axis=0)

    def softmax_pv(j, slot, c, lo):
        vT = vT_ref[0, 0, :, pl.ds(pl.multiple_of(j * tk, tk), tk)]
        m_prev = m_ref[c, :, lo:lo + half]
        m_new = jnp.maximum(m_prev, _sublane_all(jnp.maximum, mx_bufs[slot][c, :, lo:lo + half]))
        s3 = s_bufs[slot][c, :, lo:lo + half].reshape(tk // 8, 8, half)
        p = jnp.exp2(s3 - m_new[None]).reshape(tk, half).astype(_BF16)
        pv = _dot(vT, p).reshape(VT_ROWS // 8, 8, half)
        alpha = jnp.exp2(m_prev - m_new)
        acc_ref[c, :, :, lo:lo + half] = alpha[None] * acc_ref[c, :, :, lo:lo + half] + pv
        m_ref[c, :, lo:lo + half] = m_new

    def block(sc, pv):
        units = lambda lanes: [(c, lanes[0] + i * half) for c in range(2) for i in range(lanes[1] // half)]
        sc_units = units(sc[3]) if sc else []
        pv_units = units(pv[2]) if pv else []
        for idx in range(max(len(sc_units), len(pv_units))):
            if idx < len(sc_units):
                scores(sc[0], sc[1], sc[2], sc[4], *sc_units[idx])
            if idx < len(pv_units):
                softmax_pv(pv[0], pv[1], *pv_units[idx])

    qz = split_maps(qi)
    qz_ref[0] = qz[0]
    qz_ref[1] = qz[1]
    m_ref[...] = jnp.full(m_ref.shape, -jnp.inf, _F32)
    acc_ref[...] = jnp.zeros(acc_ref.shape, _F32)
    n_tiles = 2 * qi + 2

    def finish():
        block((n_tiles - 1, 1, True, upper, None), (n_tiles - 2, 0, full))
        q_next = split_maps(jnp.minimum(qi + 1, n_q - 1))
        block((0, 0, False, full, q_next), (n_tiles - 1, 1, upper))
        lam = _lambda_full(lam_ref, lam_init).reshape(1, 1, 1)
        n_v = V_DIM // 8
        l1 = _sublane_all(jnp.add, acc_ref[0, n_v])
        l2 = _sublane_all(jnp.add, acc_ref[1, n_v])
        oT = acc_ref[0, 0:n_v] / l1[None] - lam * (acc_ref[1, 0:n_v] / l2[None])
        ms = _sublane_all(jnp.add, jnp.sum(oT * oT, axis=0)) * (1.0 / V_DIM)
        oT = oT * lax.rsqrt(ms + SUBLN_EPS)[None]
        oT = oT.reshape(V_DIM, tq) * g_ref[...] * (1.0 - lam_init)
        o_ref[0] = (sg_ref[0].astype(_F32) * oT.T).astype(_BF16)

    @pl.when(qi == 0)
    def _():
        block((0, 0, True, full, None), None)
        finish()

    def pair(p):
        block((2 * p + 1, 1, False, full, None), (2 * p, 0, full))
        block((2 * p + 2, 0, False, full, None), (2 * p + 1, 1, full))

    def four_pairs(i, carry):
        for u in range(4):
            pair(4 * i + u)
        return carry

    n_pairs = jnp.maximum(qi - 1, 0)
    lax.fori_loop(0, lax.shift_right_logical(n_pairs, 2), four_pairs, 0)
    n_rest = n_pairs & 3

    @pl.when(n_rest >= 2)
    def _():
        pair(n_pairs - n_rest)
        pair(n_pairs - n_rest + 1)

    @pl.when((n_rest & 1) == 1)
    def _():
        pair(n_pairs - 1)

    @pl.when(qi > 0)
    def _():
        block((n_tiles - 3, 1, False, full, None), (n_tiles - 4, 0, full))
        block((n_tiles - 2, 0, True, full, None), (n_tiles - 3, 1, full))
        finish()


def _flash_attention(qT_hm, k_hm, vT_hm, sg, lam_p, subln_g_col, *, tq, lam_init):
    n_seq, n_heads, seq_len, _ = k_hm.shape
    tk = tq // 2
    assert seq_len % tq == 0 and tk % CHUNK == 0 and tk % 128 == 0
    row_spec = pl.BlockSpec((1, tq, V_DIM), lambda b, h, i: (b, i, h))
    return pl.pallas_call(
        functools.partial(_flash_kernel, tq=tq, tk=tk, n_q=seq_len // tq, lam_init=lam_init),
        grid=(n_seq, n_heads, seq_len // tq),
        in_specs=[
            pl.BlockSpec((1, 1, V_DIM, seq_len), lambda b, h, i: (b, h, 0, 0)),
            pl.BlockSpec((1, 1, seq_len, V_DIM), lambda b, h, i: (b, h, 0, 0)),
            pl.BlockSpec((1, 1, VT_ROWS, seq_len), lambda b, h, i: (b, h, 0, 0)),
            row_spec,
            pl.BlockSpec(lam_p.shape, lambda b, h, i: (0, 0)),
            pl.BlockSpec(subln_g_col.shape, lambda b, h, i: (0, 0)),
        ],
        out_specs=row_spec,
        out_shape=jax.ShapeDtypeStruct(sg.shape, _BF16),
        scratch_shapes=[pltpu.VMEM((2, V_DIM, tq), _BF16),
                        pltpu.VMEM((2, tk, tq), _F32), pltpu.VMEM((2, tk, tq), _F32),
                        pltpu.VMEM((2, 8, tq), _F32), pltpu.VMEM((2, 8, tq), _F32),
                        pltpu.VMEM((2, 8, tq), _F32),
                        pltpu.VMEM((2, VT_ROWS // 8, 8, tq), _F32)],
        compiler_params=pltpu.CompilerParams(
            dimension_semantics=("arbitrary", "arbitrary", "arbitrary"),
            vmem_limit_bytes=V7X_VMEM_LIMIT_BYTES),
        name="flash_diff_attn",
    )(qT_hm, k_hm, vT_hm, sg, lam_p, subln_g_col)


def _decode_attn_kernel(q_ref, k_ref, v_ref, ck_ref, cv_ref, sg_ref, lam_ref, g_ref, o_ref,
                        *, n_heads, vis_past, vis_new, lam_init):
    lam = _lambda_full(lam_ref, lam_init)
    for h in range(n_heads):
        cs = slice(h * V_DIM, (h + 1) * V_DIM)
        qs = _split_maps(q_ref[0, h])
        k_new, v_new = k_ref[0, h], v_ref[0, h]
        k_past = ck_ref[0, h].astype(_BF16)
        v_past = cv_ref[0, h].astype(_BF16)
        outs = []
        for c in range(2):
            s_past = _dot_nt(qs[c], k_past)
            s_new = _dot_nt(qs[c], k_new)
            if vis_past is not None:
                s_past = jnp.where(vis_past, s_past, -jnp.inf)
            if vis_new is not None:
                s_new = jnp.where(vis_new, s_new, -jnp.inf)
            m = jnp.maximum(jnp.max(s_past, axis=1, keepdims=True),
                            jnp.max(s_new, axis=1, keepdims=True))
            p_past = jnp.exp2(s_past - m)
            p_new = jnp.exp2(s_new - m)
            l = jnp.sum(p_past, axis=1, keepdims=True) + jnp.sum(p_new, axis=1, keepdims=True)
            o = _dot(p_past.astype(_BF16), v_past) + _dot(p_new.astype(_BF16), v_new)
            outs += [o, l]
        o_ref[0, :, cs] = _finish_heads(*outs, lam, lam_init, g_ref[...], sg_ref[0, :, cs])


def _static_visibility(q_pos, k_pos):
    vis = (k_pos[None, :] // CHUNK) <= (q_pos[:, None] // CHUNK)
    return None if vis.all() else jnp.asarray(vis)


def _decode_attention(q_hm, k_hm, v_hm, ck, cv, sg, lam_p, subln_g, *, lam_init):
    n_seq, n_heads, t_new, _ = q_hm.shape
    past = ck.shape[2]
    q_pos = past + np.arange(t_new)
    vis_past = _static_visibility(q_pos, np.arange(past))
    vis_new = _static_visibility(q_pos, q_pos)
    new_spec = pl.BlockSpec((1, n_heads, t_new, V_DIM), lambda b: (b, 0, 0, 0))
    cache_spec = pl.BlockSpec((1, n_heads, past, V_DIM), lambda b: (b, 0, 0, 0))
    row_spec = pl.BlockSpec((1, t_new, n_heads * V_DIM), lambda b: (b, 0, 0))
    return pl.pallas_call(
        functools.partial(_decode_attn_kernel, n_heads=n_heads, vis_past=vis_past, vis_new=vis_new,
                          lam_init=lam_init),
        grid=(n_seq,),
        in_specs=[new_spec, new_spec, new_spec, cache_spec, cache_spec, row_spec,
                  pl.BlockSpec(lam_p.shape, lambda b: (0, 0)),
                  pl.BlockSpec(subln_g.shape, lambda b: (0, 0))],
        out_specs=row_spec,
        out_shape=jax.ShapeDtypeStruct(sg.shape, _BF16),
        compiler_params=pltpu.CompilerParams(
            dimension_semantics=("arbitrary",), vmem_limit_bytes=V7X_VMEM_LIMIT_BYTES),
        name="decode_diff_attn",
    )(q_hm, k_hm, v_hm, ck, cv, sg, lam_p, subln_g)


def _out_kernel(x_ref, mp_ref, ma_ref, w_ref, g_ref, b_ref, y_ref, *, pool_width, alpha, chunk):
    for r0 in range(0, x_ref.shape[0], chunk):
        rs = slice(r0, r0 + chunk)
        out = _dot(mp_ref[rs, :], w_ref[0:pool_width, :]) + _dot(ma_ref[rs, :], w_ref[pool_width:, :])
        z = alpha * x_ref[rs, :] + out
        mu = jnp.mean(z, axis=-1, keepdims=True)
        zc = z - mu
        var = jnp.mean(zc * zc, axis=-1, keepdims=True)
        y_ref[rs, :] = zc * lax.rsqrt(var + LN_EPS) * g_ref[...] + b_ref[...]


def _out_proj(x2d, mp, ma, w_out_b, ln_g, ln_b, *, tm, alpha):
    n_rows, d_model = x2d.shape
    pw, aw = mp.shape[1], ma.shape[1]
    assert n_rows % tm == 0
    row = lambda i: (i, 0)
    const = lambda i: (0, 0)
    return pl.pallas_call(
        functools.partial(_out_kernel, pool_width=pw, alpha=alpha, chunk=min(tm, OUT_ROW_CHUNK)),
        grid=(n_rows // tm,),
        in_specs=[pl.BlockSpec((tm, d_model), row), pl.BlockSpec((tm, pw), row),
                  pl.BlockSpec((tm, aw), row), pl.BlockSpec(w_out_b.shape, const),
                  pl.BlockSpec((1, d_model), const), pl.BlockSpec((1, d_model), const)],
        out_specs=pl.BlockSpec((tm, d_model), row),
        out_shape=jax.ShapeDtypeStruct((n_rows, d_model), _F32),
        compiler_params=pltpu.CompilerParams(
            dimension_semantics=("arbitrary",), vmem_limit_bytes=V7X_VMEM_LIMIT_BYTES),
        name="out_proj_ln",
    )(x2d, mp, ma, w_out_b, ln_g, ln_b)


def _block_diag(pool_w):
    g, c, d = pool_w.shape
    out = jnp.zeros((g * c, g * d), pool_w.dtype)
    for i in range(g):
        out = out.at[i * c:(i + 1) * c, i * d:(i + 1) * d].set(pool_w[i])
    return out


def _swap_seq_head(rows):
    return jnp.transpose(rows, (0, 2, 1, 3))


def _layer(x, hist, n_valid, k_past, v_past, lam_init, alpha, params, *, rows, tq, tm_out):
    (w_in, pool_w, pool_b, pool_scale, lq1, lk1, lq2, lk2, subln_g, w_out, ln_g, ln_b) = params
    n_seq, seq_len, d_model = x.shape
    pw = pool_scale.shape[0]
    aw = w_out.shape[0] - pw
    n_heads = aw // V_DIM
    x2d = x.reshape(n_seq * seq_len, d_model)
    w_in_b, w_out_b = w_in.astype(_BF16), w_out.astype(_BF16)
    pw_bd = _block_diag(pool_w).astype(_BF16)
    lam_p = jnp.stack([lq1, lk1, lq2, lk2]).astype(_F32)
    prompt = k_past is None
    nb = 1 if prompt else n_seq
    krow, vrow, q_hm, k_hm, v_hm, sg, mp, pst = _in_proj(
        x2d, hist, w_in_b, pw_bd, pool_b.reshape(1, pw).astype(_F32),
        pool_scale.reshape(1, pw).astype(_F32),
        n_seq=n_seq, seq_len=seq_len, nb=nb, rows=rows, n_valid=n_valid, transposed=prompt)
    sg3 = sg.reshape(n_seq, seq_len, aw)
    if prompt:
        ma = _flash_attention(q_hm, k_hm, v_hm, sg3, lam_p, subln_g.reshape(V_DIM, 1).astype(_F32),
                              tq=tq, lam_init=lam_init)
    else:
        ma = _decode_attention(q_hm, k_hm, v_hm, _swap_seq_head(k_past), _swap_seq_head(v_past), sg3,
                               lam_p, subln_g.reshape(1, V_DIM).astype(_F32), lam_init=lam_init)
    y = _out_proj(x2d, mp, ma.reshape(n_seq * seq_len, aw), w_out_b,
                  ln_g.reshape(1, d_model).astype(_F32), ln_b.reshape(1, d_model).astype(_F32),
                  tm=tm_out, alpha=alpha)
    return y.reshape(n_seq, seq_len, d_model), _swap_seq_head(krow), _swap_seq_head(vrow), pst


def kernel(x_prompt, x_sample, cache_k, cache_v, state_pool, w_in, pool_w, pool_b, pool_scale,
           lambda_q1, lambda_k1, lambda_q2, lambda_k2, subln_g, w_out, ln_g, ln_b):
    depth = w_in.shape[0]
    pw = pool_scale.shape[1]
    pool_hist = max(POOL_WINDOWS) - 1
    alpha = (2.0 * depth) ** 0.25
    n_valid_sample = min(cache_k.shape[2], pool_hist)
    yp, ys = x_prompt, x_sample
    outs = [[] for _ in range(6)]
    for l in range(depth):
        lam_init = _lambda_init(l)
        params = (w_in[l], pool_w[l], pool_b[l], pool_scale[l], lambda_q1[l], lambda_k1[l],
                  lambda_q2[l], lambda_k2[l], subln_g[l], w_out[l], ln_g[l], ln_b[l])
        zero_hist = jnp.zeros((yp.shape[0], HIST_ROWS, pw), _F32)
        yp, kp, vp, pp = _layer(yp, zero_hist, 0, None, None, lam_init, alpha, params,
                                rows=512, tq=1024, tm_out=1024)
        hist_s = jnp.pad(state_pool[l].astype(_F32), ((0, 0), (HIST_ROWS - pool_hist, 0), (0, 0)))
        ys, kn, vn, pn = _layer(ys, hist_s, n_valid_sample, cache_k[l], cache_v[l], lam_init, alpha,
                                params, rows=ys.shape[1], tq=None,
                                tm_out=ys.shape[0] * ys.shape[1])
        for lst, val in zip(outs, (kp, vp, pp[:, HIST_ROWS - pool_hist:], kn, vn,
                                   pn[:, HIST_ROWS - pool_hist:])):
            lst.append(val)
    return (yp, ys) + tuple(o[0][None] if depth == 1 else jnp.stack(o) for o in outs)
```

```python
import functools
import math

import numpy as np
import jax
import jax.numpy as jnp
from jax import lax
from jax.experimental import pallas as pl
from jax.experimental.pallas import tpu as pltpu

CHUNK = 64
POOL_WINDOWS = (2, 4, 8, 16)
POOL_GROUPS = len(POOL_WINDOWS)
HEAD_DIM = 64
V_DIM = 2 * HEAD_DIM
ATTN_SCALE = HEAD_DIM ** -0.5
Q_SCALE = ATTN_SCALE * math.log2(math.e)
LN_EPS = 1e-5
SUBLN_EPS = 1e-5

BF16_ROW_TILE = 16
VT_ROWS = V_DIM + BF16_ROW_TILE

HIST_ROWS = 16
V7X_VMEM_LIMIT_BYTES = 56 * 1024 * 1024
OUT_ROW_CHUNK = 256

_BF16 = jnp.bfloat16
_F32 = jnp.float32


def _lambda_init(layer_idx):
    return 0.8 - 0.6 * math.exp(-0.3 * layer_idx)


def _sigmoid(x):
    return 1.0 / (1.0 + jnp.exp(-x))


def _dot(a, b):
    return jnp.dot(a, b, preferred_element_type=_F32)


def _dot_nt(a, b):
    return lax.dot_general(a, b, (((1,), (1,)), ((), ())), preferred_element_type=_F32)


def _proj_kernel(x_ref, hist_ref, w_ref, pw_ref, pb_ref, ps_ref,
                 krow_ref, vrow_ref, q_ref, k_ref, v_ref, sg_ref, mp_ref, pst_ref,
                 carry_ref, *, nb, rows, n_valid, pool_width, attn_width, n_heads, transposed):
    i = pl.program_id(1)
    pw, aw = pool_width, attn_width
    group_dim = pw // POOL_GROUPS

    @pl.when(i == 0)
    def _():
        carry_ref[...] = hist_ref[...]

    xb = x_ref[...].astype(_BF16)
    u = _dot(xb, w_ref[:, 0:pw])
    g_pool = _dot(xb, w_ref[:, pw:2 * pw])
    c0 = 2 * pw

    def head_blocks(val, write):
        for b in range(nb):
            for h in range(n_heads):
                write(b, h, val[b * rows:(b + 1) * rows, h * V_DIM:(h + 1) * V_DIM])

    def write_q(b, h, blk):
        q_ref[b, h] = (blk.T if transposed else blk).astype(_BF16)

    def write_k(b, h, blk):
        krow_ref[b, h] = blk
        k_ref[b, h] = blk.astype(_BF16)

    def write_v(b, h, blk):
        vrow_ref[b, h] = blk
        if transposed:
            v_ref[b, h, 0:V_DIM, :] = blk.T.astype(_BF16)
            ones_row = lax.broadcasted_iota(jnp.int32, (BF16_ROW_TILE, rows), 0) == 0
            v_ref[b, h, V_DIM:VT_ROWS, :] = ones_row.astype(_BF16)
        else:
            v_ref[b, h] = blk.astype(_BF16)

    head_blocks(_dot(xb, w_ref[:, c0:c0 + aw]) * Q_SCALE, write_q)

    ext_rows = HIST_ROWS + rows
    lane = lax.broadcasted_iota(jnp.int32, (ext_rows, pw), 1)
    row = lax.broadcasted_iota(jnp.int32, (ext_rows, pw), 0)
    t_pos = i * rows + row - HIST_ROWS
    win = jnp.full((ext_rows, pw), POOL_WINDOWS[-1], jnp.int32)
    for g in range(POOL_GROUPS - 2, -1, -1):
        win = jnp.where(lane < (g + 1) * group_dim, POOL_WINDOWS[g], win)
    cnt = jnp.maximum(jnp.minimum(win, n_valid + t_pos + 1), 1).astype(_F32)
    pooled_parts = []
    for b in range(nb):
        u_b = u[b * rows:(b + 1) * rows]
        ext = jnp.concatenate([carry_ref[b], u_b], axis=0)
        sums = {1: ext}
        w = 1
        while w < POOL_WINDOWS[-1]:
            sums[2 * w] = sums[w] + pltpu.roll(sums[w], shift=w, axis=0)
            w *= 2
        sel = sums[POOL_WINDOWS[-1]]
        for g in range(POOL_GROUPS - 2, -1, -1):
            sel = jnp.where(lane < (g + 1) * group_dim, sums[POOL_WINDOWS[g]], sel)
        pooled_parts.append((sel / cnt - ext)[HIST_ROWS:])
        carry_ref[b] = u_b[rows - HIST_ROWS:]
        pst_ref[b] = u_b[rows - HIST_ROWS:]
    pooled = pooled_parts[0] if nb == 1 else jnp.concatenate(pooled_parts, axis=0)
    mixed = (_dot(pooled.astype(_BF16), pw_ref[...]) + pb_ref[...]) * ps_ref[...]
    mp_ref[...] = (g_pool * _sigmoid(g_pool) * mixed).astype(_BF16)

    head_blocks(_dot(xb, w_ref[:, c0 + aw:c0 + 2 * aw]), write_k)
    head_blocks(_dot(xb, w_ref[:, c0 + 2 * aw:c0 + 3 * aw]), write_v)
    g_attn = _dot(xb, w_ref[:, c0 + 3 * aw:c0 + 4 * aw])
    sg_ref[...] = (g_attn * _sigmoid(g_attn)).astype(_BF16)


def _in_proj(x2d, hist, w_in_b, pw_bd, pool_b, pool_scale, *, n_seq, seq_len, nb, rows, n_valid,
             transposed):
    d_model = x2d.shape[1]
    pw = pw_bd.shape[0]
    aw = (w_in_b.shape[1] - 2 * pw) // 4
    n_heads = aw // V_DIM
    assert n_seq % nb == 0 and seq_len % rows == 0 and rows >= HIST_ROWS and rows % 8 == 0
    assert nb == 1 or rows == seq_len
    n_tiles = seq_len // rows
    tm = nb * rows
    grid = (n_seq // nb, n_tiles)
    row_map = lambda s, i: (s * n_tiles + i, 0)
    const2 = lambda s, i: (0, 0)
    hm_shape = jax.ShapeDtypeStruct((n_seq, n_heads, seq_len, V_DIM), _BF16)
    hm_spec = pl.BlockSpec((nb, n_heads, rows, V_DIM), lambda s, i: (s, 0, i, 0))
    if transposed:
        q_shape = jax.ShapeDtypeStruct((n_seq, n_heads, V_DIM, seq_len), _BF16)
        q_spec = pl.BlockSpec((nb, n_heads, V_DIM, rows), lambda s, i: (s, 0, 0, i))
        v_shape = jax.ShapeDtypeStruct((n_seq, n_heads, VT_ROWS, seq_len), _BF16)
        v_spec = pl.BlockSpec((nb, n_heads, VT_ROWS, rows), lambda s, i: (s, 0, 0, i))
    else:
        q_shape, q_spec, v_shape, v_spec = hm_shape, hm_spec, hm_shape, hm_spec
    n_rows = n_seq * seq_len
    rows_shape = jax.ShapeDtypeStruct((n_seq, n_heads, seq_len, V_DIM), _F32)
    rows_spec = hm_spec
    kernel = functools.partial(_proj_kernel, nb=nb, rows=rows, n_valid=n_valid, pool_width=pw,
                               attn_width=aw, n_heads=n_heads, transposed=transposed)
    return pl.pallas_call(
        kernel,
        grid=grid,
        in_specs=[
            pl.BlockSpec((tm, d_model), row_map),
            pl.BlockSpec((nb, HIST_ROWS, pw), lambda s, i: (s, 0, 0)),
            pl.BlockSpec(w_in_b.shape, const2),
            pl.BlockSpec(pw_bd.shape, const2),
            pl.BlockSpec((1, pw), const2),
            pl.BlockSpec((1, pw), const2),
        ],
        out_specs=[
            rows_spec, rows_spec,
            q_spec, hm_spec, v_spec,
            pl.BlockSpec((tm, aw), row_map),
            pl.BlockSpec((tm, pw), row_map),
            pl.BlockSpec((nb, HIST_ROWS, pw), lambda s, i: (s, 0, 0)),
        ],
        out_shape=[
            rows_shape, rows_shape,
            q_shape, hm_shape, v_shape,
            jax.ShapeDtypeStruct((n_rows, aw), _BF16),
            jax.ShapeDtypeStruct((n_rows, pw), _BF16),
            jax.ShapeDtypeStruct((n_seq, HIST_ROWS, pw), _F32),
        ],
        scratch_shapes=[pltpu.VMEM((nb, HIST_ROWS, pw), _F32)],
        compiler_params=pltpu.CompilerParams(
            dimension_semantics=("arbitrary", "arbitrary"),
            vmem_limit_bytes=V7X_VMEM_LIMIT_BYTES),
        name="in_proj_pool",
    )(x2d, hist, w_in_b, pw_bd, pool_b, pool_scale)


def _lambda_full(lam_ref, lam_init):
    lp = lam_ref[...]
    a1 = jnp.sum(lp[0:1] * lp[1:2], axis=1, keepdims=True)
    a2 = jnp.sum(lp[2:3] * lp[3:4], axis=1, keepdims=True)
    return jnp.exp(a1) - jnp.exp(a2) + lam_init


def _split_maps(q):
    lane = lax.broadcasted_iota(jnp.int32, q.shape, 1)
    zero = jnp.zeros_like(q)
    return jnp.where(lane < HEAD_DIM, q, zero), jnp.where(lane >= HEAD_DIM, q, zero)


def _finish_heads(o1, l1, o2, l2, lam, lam_init, g, sg):
    o = o1 / l1 - lam * (o2 / l2)
    o = o * lax.rsqrt(jnp.mean(o * o, axis=-1, keepdims=True) + SUBLN_EPS)
    o = o * g * (1.0 - lam_init)
    return (sg.astype(_F32) * o).astype(_BF16)


def _sublane_all(op, x):
    for shift in (4, 2, 1):
        x = op(x, pltpu.roll(x, shift=shift, axis=0))
    return x


def _flash_kernel(qT_ref, k_ref, vT_ref, sg_ref, lam_ref, g_ref, o_ref,
                  qz_ref, s0_ref, s1_ref, mx0_ref, mx1_ref, m_ref, acc_ref, bias_ref,
                  *, tq, tk, n_q, lam_init):
    qi = pl.program_id(2)
    s_bufs, mx_bufs = (s0_ref, s1_ref), (mx0_ref, mx1_ref)
    half = tq // 2
    full, upper = (0, tq), (half, half)

    def split_maps(q_tile):
        qT = qT_ref[0, 0, :, pl.ds(pl.multiple_of(q_tile * tq, tq), tq)]
        row = lax.broadcasted_iota(jnp.int32, qT.shape, 0)
        zero = jnp.zeros_like(qT)
        return jnp.where(row < HEAD_DIM, qT, zero), jnp.where(row >= HEAD_DIM, qT, zero)

    @pl.when(jnp.logical_and(jnp.logical_and(pl.program_id(0) == 0, pl.program_id(1) == 0), qi == 0))
    def _():
        k_rel = lax.broadcasted_iota(jnp.int32, (tk, half), 0)
        q_rel = lax.broadcasted_iota(jnp.int32, (tk, half), 1)
        bias_ref[...] = jnp.where(k_rel <= (q_rel | (CHUNK - 1)), 0.0, -jnp.inf).astype(_F32)

    def scores(j, slot, diag_lo, q_next, c, lo):
        start = pl.multiple_of(j * tk, tk)
        k = k_ref[0, 0, pl.ds(start, tk), :]
        q_c = qz_ref[c, :, lo:lo + half] if q_next is None else q_next[c][:, lo:lo + half]
        sT = _dot(k, q_c)
        if diag_lo == lo:
            sT = sT + bias_ref[...]
        s_bufs[slot][c, :, lo:lo + half] = sT
        mx_bufs[slot][c, :, lo:lo + half] = jnp.max(sT.reshape(tk // 8, 8, half), axis=0)

    def softmax_pv(j, slot, c, lo):
        vT = vT_ref[0, 0, :, pl.ds(pl.multiple_of(j * tk, tk), tk)]
        m_prev = m_ref[c, :, lo:lo + half]
        m_new = jnp.maximum(m_prev, _sublane_all(jnp.maximum, mx_bufs[slot][c, :, lo:lo + half]))
        s3 = s_bufs[slot][c, :, lo:lo + half].reshape(tk // 8, 8, half)
        p = jnp.exp2(s3 - m_new[None]).reshape(tk, half).astype(_BF16)
        pv = _dot(vT, p).reshape(VT_ROWS // 8, 8, half)
        alpha = jnp.exp2(m_prev - m_new)
        acc_ref[c, :, :, lo:lo + half] = alpha[None] * acc_ref[c, :, :, lo:lo + half] + pv
        m_ref[c, :, lo:lo + half] = m_new

    def block(sc, pv):
        units = lambda lanes: [(c, lanes[0] + i * half) for c in range(2) for i in range(lanes[1] // half)]
        sc_units = units(sc[3]) if sc else []
        pv_units = units(pv[2]) if pv else []
        for idx in range(max(len(sc_units), len(pv_units))):
            if idx < len(sc_units):
                scores(sc[0], sc[1], sc[2], sc[4], *sc_units[idx])
            if idx < len(pv_units):
                softmax_pv(pv[0], pv[1], *pv_units[idx])

    qz = split_maps(qi)
    qz_ref[0] = qz[0]
    qz_ref[1] = qz[1]
    m_ref[...] = jnp.full(m_ref.shape, -jnp.inf, _F32)
    acc_ref[...] = jnp.zeros(acc_ref.shape, _F32)
    n_tiles = 2 * qi + 2

    def finish():
        block((n_tiles - 1, 1, half, upper, None), (n_tiles - 2, 0, full))
        q_next = split_maps(jnp.minimum(qi + 1, n_q - 1))
        block((0, 0, None, full, q_next), (n_tiles - 1, 1, upper))
        lam = _lambda_full(lam_ref, lam_init).reshape(1, 1, 1)
        n_v = V_DIM // 8
        l1 = _sublane_all(jnp.add, acc_ref[0, n_v])
        l2 = _sublane_all(jnp.add, acc_ref[1, n_v])
        oT = acc_ref[0, 0:n_v] * (1.0 / l1)[None] - acc_ref[1, 0:n_v] * (lam[0] / l2)[None]
        ms = _sublane_all(jnp.add, jnp.sum(oT * oT, axis=0)) * (1.0 / V_DIM)
        oT = oT * lax.rsqrt(ms + SUBLN_EPS)[None]
        oT = oT.reshape(V_DIM, tq) * g_ref[...] * (1.0 - lam_init)
        o_ref[0] = (sg_ref[0].astype(_F32) * oT.T).astype(_BF16)

    @pl.when(qi == 0)
    def _():
        block((0, 0, 0, full, None), None)
        finish()

    def pair(p):
        block((2 * p + 1, 1, None, full, None), (2 * p, 0, full))
        block((2 * p + 2, 0, None, full, None), (2 * p + 1, 1, full))

    def four_pairs(i, carry):
        for u in range(4):
            pair(4 * i + u)
        return carry

    n_pairs = jnp.maximum(qi - 1, 0)
    lax.fori_loop(0, lax.shift_right_logical(n_pairs, 2), four_pairs, 0)
    n_rest = n_pairs & 3

    @pl.when(n_rest >= 2)
    def _():
        pair(n_pairs - n_rest)
        pair(n_pairs - n_rest + 1)

    @pl.when((n_rest & 1) == 1)
    def _():
        pair(n_pairs - 1)

    @pl.when(qi > 0)
    def _():
        block((n_tiles - 3, 1, None, full, None), (n_tiles - 4, 0, full))
        block((n_tiles - 2, 0, 0, full, None), (n_tiles - 3, 1, full))
        finish()


def _flash_attention(qT_hm, k_hm, vT_hm, sg, lam_p, subln_g_col, *, tq, lam_init):
    n_seq, n_heads, seq_len, _ = k_hm.shape
    tk = tq // 2
    assert seq_len % tq == 0 and tk % CHUNK == 0 and tk % 128 == 0
    row_spec = pl.BlockSpec((1, tq, V_DIM), lambda b, h, i: (b, i, h))
    return pl.pallas_call(
        functools.partial(_flash_kernel, tq=tq, tk=tk, n_q=seq_len // tq, lam_init=lam_init),
        grid=(n_seq, n_heads, seq_len // tq),
        in_specs=[
            pl.BlockSpec((1, 1, V_DIM, seq_len), lambda b, h, i: (b, h, 0, 0)),
            pl.BlockSpec((1, 1, seq_len, V_DIM), lambda b, h, i: (b, h, 0, 0)),
            pl.BlockSpec((1, 1, VT_ROWS, seq_len), lambda b, h, i: (b, h, 0, 0)),
            row_spec,
            pl.BlockSpec(lam_p.shape, lambda b, h, i: (0, 0)),
            pl.BlockSpec(subln_g_col.shape, lambda b, h, i: (0, 0)),
        ],
        out_specs=row_spec,
        out_shape=jax.ShapeDtypeStruct(sg.shape, _BF16),
        scratch_shapes=[pltpu.VMEM((2, V_DIM, tq), _BF16),
                        pltpu.VMEM((2, tk, tq), _F32), pltpu.VMEM((2, tk, tq), _F32),
                        pltpu.VMEM((2, 8, tq), _F32), pltpu.VMEM((2, 8, tq), _F32),
                        pltpu.VMEM((2, 8, tq), _F32),
                        pltpu.VMEM((2, VT_ROWS // 8, 8, tq), _F32),
                        pltpu.VMEM((tk, tq // 2), _F32)],
        compiler_params=pltpu.CompilerParams(
            dimension_semantics=("arbitrary", "arbitrary", "arbitrary"),
            vmem_limit_bytes=V7X_VMEM_LIMIT_BYTES),
        name="flash_diff_attn",
    )(qT_hm, k_hm, vT_hm, sg, lam_p, subln_g_col)


def _decode_attn_kernel(q_ref, k_ref, v_ref, ck_ref, cv_ref, sg_ref, lam_ref, g_ref, o_ref,
                        *, n_heads, vis_past, vis_new, lam_init):
    lam = _lambda_full(lam_ref, lam_init)
    for h in range(n_heads):
        cs = slice(h * V_DIM, (h + 1) * V_DIM)
        q2 = jnp.concatenate(_split_maps(q_ref[0, h]), axis=0)
        k_new, v_new = k_ref[0, h], v_ref[0, h]
        k_past = ck_ref[0, h].astype(_BF16)
        v_past = cv_ref[0, h].astype(_BF16)
        s_past = _dot_nt(q2, k_past)
        s_new = _dot_nt(q2, k_new)
        if vis_past is not None:
            s_past = jnp.where(jnp.concatenate([vis_past, vis_past], axis=0), s_past, -jnp.inf)
        if vis_new is not None:
            s_new = jnp.where(jnp.concatenate([vis_new, vis_new], axis=0), s_new, -jnp.inf)
        m = jnp.maximum(jnp.max(s_past, axis=1, keepdims=True), jnp.max(s_new, axis=1, keepdims=True))
        p_past = jnp.exp2(s_past - m)
        p_new = jnp.exp2(s_new - m)
        l = jnp.sum(p_past, axis=1, keepdims=True) + jnp.sum(p_new, axis=1, keepdims=True)
        o = _dot(p_past.astype(_BF16), v_past) + _dot(p_new.astype(_BF16), v_new)
        t = o.shape[0] // 2
        o_ref[0, :, cs] = _finish_heads(o[:t], l[:t], o[t:], l[t:], lam, lam_init, g_ref[...],
                                        sg_ref[0, :, cs])


def _static_visibility(q_pos, k_pos):
    vis = (k_pos[None, :] // CHUNK) <= (q_pos[:, None] // CHUNK)
    return None if vis.all() else jnp.asarray(vis)


def _decode_attention(q_hm, k_hm, v_hm, ck, cv, sg, lam_p, subln_g, *, lam_init):
    n_seq, n_heads, t_new, _ = q_hm.shape
    past = ck.shape[2]
    q_pos = past + np.arange(t_new)
    vis_past = _static_visibility(q_pos, np.arange(past))
    vis_new = _static_visibility(q_pos, q_pos)
    new_spec = pl.BlockSpec((1, n_heads, t_new, V_DIM), lambda b: (b, 0, 0, 0))
    cache_spec = pl.BlockSpec((1, n_heads, past, V_DIM), lambda b: (b, 0, 0, 0))
    row_spec = pl.BlockSpec((1, t_new, n_heads * V_DIM), lambda b: (b, 0, 0))
    return pl.pallas_call(
        functools.partial(_decode_attn_kernel, n_heads=n_heads, vis_past=vis_past, vis_new=vis_new,
                          lam_init=lam_init),
        grid=(n_seq,),
        in_specs=[new_spec, new_spec, new_spec, cache_spec, cache_spec, row_spec,
                  pl.BlockSpec(lam_p.shape, lambda b: (0, 0)),
                  pl.BlockSpec(subln_g.shape, lambda b: (0, 0))],
        out_specs=row_spec,
        out_shape=jax.ShapeDtypeStruct(sg.shape, _BF16),
        compiler_params=pltpu.CompilerParams(
            dimension_semantics=("arbitrary",), vmem_limit_bytes=V7X_VMEM_LIMIT_BYTES),
        name="decode_diff_attn",
    )(q_hm, k_hm, v_hm, ck, cv, sg, lam_p, subln_g)


def _out_kernel(x_ref, mp_ref, ma_ref, w_ref, g_ref, b_ref, y_ref, *, pool_width, alpha, chunk):
    for r0 in range(0, x_ref.shape[0], chunk):
        rs = slice(r0, r0 + chunk)
        out = _dot(mp_ref[rs, :], w_ref[0:pool_width, :]) + _dot(ma_ref[rs, :], w_ref[pool_width:, :])
        z = alpha * x_ref[rs, :] + out
        mu = jnp.mean(z, axis=-1, keepdims=True)
        zc = z - mu
        var = jnp.mean(zc * zc, axis=-1, keepdims=True)
        y_ref[rs, :] = zc * lax.rsqrt(var + LN_EPS) * g_ref[...] + b_ref[...]


def _out_proj(x2d, mp, ma, w_out_b, ln_g, ln_b, *, tm, alpha):
    n_rows, d_model = x2d.shape
    pw, aw = mp.shape[1], ma.shape[1]
    assert n_rows % tm == 0
    row = lambda i: (i, 0)
    const = lambda i: (0, 0)
    return pl.pallas_call(
        functools.partial(_out_kernel, pool_width=pw, alpha=alpha, chunk=min(tm, OUT_ROW_CHUNK)),
        grid=(n_rows // tm,),
        in_specs=[pl.BlockSpec((tm, d_model), row), pl.BlockSpec((tm, pw), row),
                  pl.BlockSpec((tm, aw), row), pl.BlockSpec(w_out_b.shape, const),
                  pl.BlockSpec((1, d_model), const), pl.BlockSpec((1, d_model), const)],
        out_specs=pl.BlockSpec((tm, d_model), row),
        out_shape=jax.ShapeDtypeStruct((n_rows, d_model), _F32),
        compiler_params=pltpu.CompilerParams(
            dimension_semantics=("arbitrary",), vmem_limit_bytes=V7X_VMEM_LIMIT_BYTES),
        name="out_proj_ln",
    )(x2d, mp, ma, w_out_b, ln_g, ln_b)


def _block_diag(pool_w):
    g, c, d = pool_w.shape
    out = jnp.zeros((g * c, g * d), pool_w.dtype)
    for i in range(g):
        out = out.at[i * c:(i + 1) * c, i * d:(i + 1) * d].set(pool_w[i])
    return out


def _swap_seq_head(rows):
    return jnp.transpose(rows, (0, 2, 1, 3))


def _layer(x, hist, n_valid, k_past, v_past, lam_init, alpha, params, *, rows, tq, tm_out):
    (w_in, pool_w, pool_b, pool_scale, lq1, lk1, lq2, lk2, subln_g, w_out, ln_g, ln_b) = params
    n_seq, seq_len, d_model = x.shape
    pw = pool_scale.shape[0]
    aw = w_out.shape[0] - pw
    n_heads = aw // V_DIM
    x2d = x.reshape(n_seq * seq_len, d_model)
    w_in_b, w_out_b = w_in.astype(_BF16), w_out.astype(_BF16)
    pw_bd = _block_diag(pool_w).astype(_BF16)
    lam_p = jnp.stack([lq1, lk1, lq2, lk2]).astype(_F32)
    prompt = k_past is None
    nb = 1 if prompt else n_seq
    krow, vrow, q_hm, k_hm, v_hm, sg, mp, pst = _in_proj(
        x2d, hist, w_in_b, pw_bd, pool_b.reshape(1, pw).astype(_F32),
        pool_scale.reshape(1, pw).astype(_F32),
        n_seq=n_seq, seq_len=seq_len, nb=nb, rows=rows, n_valid=n_valid, transposed=prompt)
    sg3 = sg.reshape(n_seq, seq_len, aw)
    if prompt:
        ma = _flash_attention(q_hm, k_hm, v_hm, sg3, lam_p, subln_g.reshape(V_DIM, 1).astype(_F32),
                              tq=tq, lam_init=lam_init)
    else:
        ma = _decode_attention(q_hm, k_hm, v_hm, _swap_seq_head(k_past), _swap_seq_head(v_past), sg3,
                               lam_p, subln_g.reshape(1, V_DIM).astype(_F32), lam_init=lam_init)
    y = _out_proj(x2d, mp, ma.reshape(n_seq * seq_len, aw), w_out_b,
                  ln_g.reshape(1, d_model).astype(_F32), ln_b.reshape(1, d_model).astype(_F32),
                  tm=tm_out, alpha=alpha)
    return y.reshape(n_seq, seq_len, d_model), _swap_seq_head(krow), _swap_seq_head(vrow), pst


def kernel(x_prompt, x_sample, cache_k, cache_v, state_pool, w_in, pool_w, pool_b, pool_scale,
           lambda_q1, lambda_k1, lambda_q2, lambda_k2, subln_g, w_out, ln_g, ln_b):
    depth = w_in.shape[0]
    pw = pool_scale.shape[1]
    pool_hist = max(POOL_WINDOWS) - 1
    alpha = (2.0 * depth) ** 0.25
    n_valid_sample = min(cache_k.shape[2], pool_hist)
    yp, ys = x_prompt, x_sample
    outs = [[] for _ in range(6)]
    for l in range(depth):
        lam_init = _lambda_init(l)
        params = (w_in[l], pool_w[l], pool_b[l], pool_scale[l], lambda_q1[l], lambda_k1[l],
                  lambda_q2[l], lambda_k2[l], subln_g[l], w_out[l], ln_g[l], ln_b[l])
        zero_hist = jnp.zeros((yp.shape[0], HIST_ROWS, pw), _F32)
        yp, kp, vp, pp = _layer(yp, zero_hist, 0, None, None, lam_init, alpha, params,
                                rows=512, tq=1024, tm_out=2048)
        hist_s = jnp.pad(state_pool[l].astype(_F32), ((0, 0), (HIST_ROWS - pool_hist, 0), (0, 0)))
        ys, kn, vn, pn = _layer(ys, hist_s, n_valid_sample, cache_k[l], cache_v[l], lam_init, alpha,
                                params, rows=ys.shape[1], tq=None,
                                tm_out=ys.shape[0] * ys.shape[1])
        for lst, val in zip(outs, (kp, vp, pp[:, HIST_ROWS - pool_hist:], kn, vn,
                                   pn[:, HIST_ROWS - pool_hist:])):
            lst.append(val)
    return (yp, ys) + tuple(o[0][None] if depth == 1 else jnp.stack(o) for o in outs)
```

```python
import functools
import math

import numpy as np
import jax
import jax.numpy as jnp
from jax import lax
from jax.experimental import pallas as pl
from jax.experimental.pallas import tpu as pltpu

CHUNK = 64
POOL_WINDOWS = (2, 4, 8, 16)
POOL_GROUPS = len(POOL_WINDOWS)
HEAD_DIM = 64
V_DIM = 2 * HEAD_DIM
ATTN_SCALE = HEAD_DIM ** -0.5
Q_SCALE = ATTN_SCALE * math.log2(math.e)
LN_EPS = 1e-5
SUBLN_EPS = 1e-5

BF16_ROW_TILE = 16
VT_ROWS = V_DIM + BF16_ROW_TILE

HIST_ROWS = 16
V7X_VMEM_LIMIT_BYTES = 56 * 1024 * 1024
OUT_ROW_CHUNK = 256

_BF16 = jnp.bfloat16
_F32 = jnp.float32


def _lambda_init(layer_idx):
    return 0.8 - 0.6 * math.exp(-0.3 * layer_idx)


def _sigmoid(x):
    return 1.0 / (1.0 + jnp.exp(-x))


def _dot(a, b):
    return jnp.dot(a, b, preferred_element_type=_F32)


def _dot_nt(a, b):
    return lax.dot_general(a, b, (((1,), (1,)), ((), ())), preferred_element_type=_F32)


def _proj_kernel(x_ref, hist_ref, w_ref, pw_ref, pb_ref, ps_ref,
                 krow_ref, vrow_ref, q_ref, k_ref, v_ref, sg_ref, mp_ref, pst_ref,
                 carry_ref, *, nb, rows, n_valid, pool_width, attn_width, n_heads, transposed):
    i = pl.program_id(1)
    pw, aw = pool_width, attn_width
    group_dim = pw // POOL_GROUPS

    @pl.when(i == 0)
    def _():
        carry_ref[...] = hist_ref[...]

    xb = x_ref[...].astype(_BF16)
    u = _dot(xb, w_ref[:, 0:pw])
    g_pool = _dot(xb, w_ref[:, pw:2 * pw])
    c0 = 2 * pw

    def head_blocks(val, write):
        for b in range(nb):
            for h in range(n_heads):
                write(b, h, val[b * rows:(b + 1) * rows, h * V_DIM:(h + 1) * V_DIM])

    def write_q(b, h, blk):
        q_ref[b, h] = (blk.T if transposed else blk).astype(_BF16)

    def write_k(b, h, blk):
        krow_ref[b, h] = blk
        k_ref[b, h] = blk.astype(_BF16)

    def write_v(b, h, blk):
        vrow_ref[b, h] = blk
        if transposed:
            v_ref[b, h, 0:V_DIM, :] = blk.T.astype(_BF16)
            ones_row = lax.broadcasted_iota(jnp.int32, (BF16_ROW_TILE, rows), 0) == 0
            v_ref[b, h, V_DIM:VT_ROWS, :] = ones_row.astype(_BF16)
        else:
            v_ref[b, h] = blk.astype(_BF16)

    head_blocks(_dot(xb, w_ref[:, c0:c0 + aw]) * Q_SCALE, write_q)

    ext_rows = HIST_ROWS + rows
    lane = lax.broadcasted_iota(jnp.int32, (ext_rows, pw), 1)
    row = lax.broadcasted_iota(jnp.int32, (ext_rows, pw), 0)
    t_pos = i * rows + row - HIST_ROWS
    win = jnp.full((ext_rows, pw), POOL_WINDOWS[-1], jnp.int32)
    for g in range(POOL_GROUPS - 2, -1, -1):
        win = jnp.where(lane < (g + 1) * group_dim, POOL_WINDOWS[g], win)
    cnt = jnp.maximum(jnp.minimum(win, n_valid + t_pos + 1), 1).astype(_F32)
    pooled_parts = []
    for b in range(nb):
        u_b = u[b * rows:(b + 1) * rows]
        ext = jnp.concatenate([carry_ref[b], u_b], axis=0)
        sums = {1: ext}
        w = 1
        while w < POOL_WINDOWS[-1]:
            sums[2 * w] = sums[w] + pltpu.roll(sums[w], shift=w, axis=0)
            w *= 2
        sel = sums[POOL_WINDOWS[-1]]
        for g in range(POOL_GROUPS - 2, -1, -1):
            sel = jnp.where(lane < (g + 1) * group_dim, sums[POOL_WINDOWS[g]], sel)
        pooled_parts.append((sel / cnt - ext)[HIST_ROWS:])
        carry_ref[b] = u_b[rows - HIST_ROWS:]
        pst_ref[b] = u_b[rows - HIST_ROWS:]
    pooled = pooled_parts[0] if nb == 1 else jnp.concatenate(pooled_parts, axis=0)
    mixed = (_dot(pooled.astype(_BF16), pw_ref[...]) + pb_ref[...]) * ps_ref[...]
    mp_ref[...] = (g_pool * _sigmoid(g_pool) * mixed).astype(_BF16)

    head_blocks(_dot(xb, w_ref[:, c0 + aw:c0 + 2 * aw]), write_k)
    head_blocks(_dot(xb, w_ref[:, c0 + 2 * aw:c0 + 3 * aw]), write_v)
    g_attn = _dot(xb, w_ref[:, c0 + 3 * aw:c0 + 4 * aw])
    sg_ref[...] = (g_attn * _sigmoid(g_attn)).astype(_BF16)


def _in_proj(x2d, hist, w_in_b, pw_bd, pool_b, pool_scale, *, n_seq, seq_len, nb, rows, n_valid,
             transposed):
    d_model = x2d.shape[1]
    pw = pw_bd.shape[0]
    aw = (w_in_b.shape[1] - 2 * pw) // 4
    n_heads = aw // V_DIM
    assert n_seq % nb == 0 and seq_len % rows == 0 and rows >= HIST_ROWS and rows % 8 == 0
    assert nb == 1 or rows == seq_len
    n_tiles = seq_len // rows
    tm = nb * rows
    grid = (n_seq // nb, n_tiles)
    row_map = lambda s, i: (s * n_tiles + i, 0)
    const2 = lambda s, i: (0, 0)
    hm_shape = jax.ShapeDtypeStruct((n_seq, n_heads, seq_len, V_DIM), _BF16)
    hm_spec = pl.BlockSpec((nb, n_heads, rows, V_DIM), lambda s, i: (s, 0, i, 0))
    if transposed:
        q_shape = jax.ShapeDtypeStruct((n_seq, n_heads, V_DIM, seq_len), _BF16)
        q_spec = pl.BlockSpec((nb, n_heads, V_DIM, rows), lambda s, i: (s, 0, 0, i))
        v_shape = jax.ShapeDtypeStruct((n_seq, n_heads, VT_ROWS, seq_len), _BF16)
        v_spec = pl.BlockSpec((nb, n_heads, VT_ROWS, rows), lambda s, i: (s, 0, 0, i))
    else:
        q_shape, q_spec, v_shape, v_spec = hm_shape, hm_spec, hm_shape, hm_spec
    n_rows = n_seq * seq_len
    rows_shape = jax.ShapeDtypeStruct((n_seq, n_heads, seq_len, V_DIM), _F32)
    rows_spec = hm_spec
    kernel = functools.partial(_proj_kernel, nb=nb, rows=rows, n_valid=n_valid, pool_width=pw,
                               attn_width=aw, n_heads=n_heads, transposed=transposed)
    return pl.pallas_call(
        kernel,
        grid=grid,
        in_specs=[
            pl.BlockSpec((tm, d_model), row_map),
            pl.BlockSpec((nb, HIST_ROWS, pw), lambda s, i: (s, 0, 0)),
            pl.BlockSpec(w_in_b.shape, const2),
            pl.BlockSpec(pw_bd.shape, const2),
            pl.BlockSpec((1, pw), const2),
            pl.BlockSpec((1, pw), const2),
        ],
        out_specs=[
            rows_spec, rows_spec,
            q_spec, hm_spec, v_spec,
            pl.BlockSpec((tm, aw), row_map),
            pl.BlockSpec((tm, pw), row_map),
            pl.BlockSpec((nb, HIST_ROWS, pw), lambda s, i: (s, 0, 0)),
        ],
        out_shape=[
            rows_shape, rows_shape,
            q_shape, hm_shape, v_shape,
            jax.ShapeDtypeStruct((n_rows, aw), _BF16),
            jax.ShapeDtypeStruct((n_rows, pw), _BF16),
            jax.ShapeDtypeStruct((n_seq, HIST_ROWS, pw), _F32),
        ],
        scratch_shapes=[pltpu.VMEM((nb, HIST_ROWS, pw), _F32)],
        compiler_params=pltpu.CompilerParams(
            dimension_semantics=("arbitrary", "arbitrary"),
            vmem_limit_bytes=V7X_VMEM_LIMIT_BYTES),
        name="in_proj_pool",
    )(x2d, hist, w_in_b, pw_bd, pool_b, pool_scale)


def _lambda_full(lam_ref, lam_init):
    lp = lam_ref[...]
    a1 = jnp.sum(lp[0:1] * lp[1:2], axis=1, keepdims=True)
    a2 = jnp.sum(lp[2:3] * lp[3:4], axis=1, keepdims=True)
    return jnp.exp(a1) - jnp.exp(a2) + lam_init


def _split_maps(q):
    lane = lax.broadcasted_iota(jnp.int32, q.shape, 1)
    zero = jnp.zeros_like(q)
    return jnp.where(lane < HEAD_DIM, q, zero), jnp.where(lane >= HEAD_DIM, q, zero)


def _finish_heads(o1, l1, o2, l2, lam, lam_init, g, sg):
    o = o1 / l1 - lam * (o2 / l2)
    o = o * lax.rsqrt(jnp.mean(o * o, axis=-1, keepdims=True) + SUBLN_EPS)
    o = o * g * (1.0 - lam_init)
    return (sg.astype(_F32) * o).astype(_BF16)


def _sublane_all(op, x):
    for shift in (4, 2, 1):
        x = op(x, pltpu.roll(x, shift=shift, axis=0))
    return x


def _flash_kernel(qT_ref, k_ref, vT_ref, sg_ref, lam_ref, g_ref, o_ref,
                  qz_ref, s0_ref, s1_ref, mx0_ref, mx1_ref, m_ref, acc_ref, bias_ref,
                  *, tq, tk, n_q, lam_init):
    qi = pl.program_id(2)
    s_bufs, mx_bufs = (s0_ref, s1_ref), (mx0_ref, mx1_ref)
    half = tq // 2
    full, upper = (0, tq), (half, half)

    def split_maps(q_tile):
        qT = qT_ref[0, 0, :, pl.ds(pl.multiple_of(q_tile * tq, tq), tq)]
        row = lax.broadcasted_iota(jnp.int32, qT.shape, 0)
        zero = jnp.zeros_like(qT)
        return jnp.where(row < HEAD_DIM, qT, zero), jnp.where(row >= HEAD_DIM, qT, zero)

    @pl.when(jnp.logical_and(jnp.logical_and(pl.program_id(0) == 0, pl.program_id(1) == 0), qi == 0))
    def _():
        k_rel = lax.broadcasted_iota(jnp.int32, (tk, half), 0)
        q_rel = lax.broadcasted_iota(jnp.int32, (tk, half), 1)
        bias_ref[...] = jnp.where(k_rel <= (q_rel | (CHUNK - 1)), 0.0, -jnp.inf).astype(_F32)

    def scores(j, slot, diag_lo, q_next, c, lo):
        start = pl.multiple_of(j * tk, tk)
        k = k_ref[0, 0, pl.ds(start, tk), :]
        q_c = qz_ref[c, :, lo:lo + half] if q_next is None else q_next[c][:, lo:lo + half]
        sT = _dot(k, q_c)
        if diag_lo == lo:
            sT = sT + bias_ref[...]
        s_bufs[slot][c, :, lo:lo + half] = sT
        mx_bufs[slot][c, :, lo:lo + half] = jnp.max(sT.reshape(tk // 8, 8, half), axis=0)

    def softmax_pv(j, slot, c, lo):
        vT = vT_ref[0, 0, :, pl.ds(pl.multiple_of(j * tk, tk), tk)]
        m_prev = m_ref[c, :, lo:lo + half]
        m_new = jnp.maximum(m_prev, _sublane_all(jnp.maximum, mx_bufs[slot][c, :, lo:lo + half]))
        s3 = s_bufs[slot][c, :, lo:lo + half].reshape(tk // 8, 8, half)
        p = jnp.exp2(s3 - m_new[None]).reshape(tk, half).astype(_BF16)
        pv = _dot(vT, p).reshape(VT_ROWS // 8, 8, half)
        alpha = jnp.exp2(m_prev - m_new)
        acc_ref[c, :, :, lo:lo + half] = alpha[None] * acc_ref[c, :, :, lo:lo + half] + pv
        m_ref[c, :, lo:lo + half] = m_new

    def block(sc, pv):
        units = lambda lanes: [(c, lanes[0] + i * half) for c in range(2) for i in range(lanes[1] // half)]
        sc_units = units(sc[3]) if sc else []
        pv_units = units(pv[2]) if pv else []
        for idx in range(max(len(sc_units), len(pv_units))):
            if idx < len(sc_units):
                scores(sc[0], sc[1], sc[2], sc[4], *sc_units[idx])
            if idx < len(pv_units):
                softmax_pv(pv[0], pv[1], *pv_units[idx])

    qz = split_maps(qi)
    qz_ref[0] = qz[0]
    qz_ref[1] = qz[1]
    m_ref[...] = jnp.full(m_ref.shape, -jnp.inf, _F32)
    acc_ref[...] = jnp.zeros(acc_ref.shape, _F32)
    n_tiles = 2 * qi + 2

    def finish():
        block((n_tiles - 1, 1, half, upper, None), (n_tiles - 2, 0, full))
        q_next = split_maps(jnp.minimum(qi + 1, n_q - 1))
        block((0, 0, None, full, q_next), (n_tiles - 1, 1, upper))
        lam = _lambda_full(lam_ref, lam_init).reshape(1, 1, 1)
        n_v = V_DIM // 8
        l1 = _sublane_all(jnp.add, acc_ref[0, n_v])
        l2 = _sublane_all(jnp.add, acc_ref[1, n_v])
        oT = acc_ref[0, 0:n_v] * (1.0 / l1)[None] - acc_ref[1, 0:n_v] * (lam[0] / l2)[None]
        ms = _sublane_all(jnp.add, jnp.sum(oT * oT, axis=0)) * (1.0 / V_DIM)
        oT = oT * lax.rsqrt(ms + SUBLN_EPS)[None]
        oT = oT.reshape(V_DIM, tq) * g_ref[...] * (1.0 - lam_init)
        o_ref[0] = (sg_ref[0].astype(_F32) * oT.T).astype(_BF16)

    @pl.when(qi == 0)
    def _():
        block((0, 0, 0, full, None), None)
        finish()

    def pair(p):
        block((2 * p + 1, 1, None, full, None), (2 * p, 0, full))
        block((2 * p + 2, 0, None, full, None), (2 * p + 1, 1, full))

    def four_pairs(i, carry):
        for u in range(4):
            pair(4 * i + u)
        return carry

    n_pairs = jnp.maximum(qi - 1, 0)
    lax.fori_loop(0, lax.shift_right_logical(n_pairs, 2), four_pairs, 0)
    n_rest = n_pairs & 3

    @pl.when(n_rest >= 2)
    def _():
        pair(n_pairs - n_rest)
        pair(n_pairs - n_rest + 1)

    @pl.when((n_rest & 1) == 1)
    def _():
        pair(n_pairs - 1)

    @pl.when(qi > 0)
    def _():
        block((n_tiles - 3, 1, None, full, None), (n_tiles - 4, 0, full))
        block((n_tiles - 2, 0, 0, full, None), (n_tiles - 3, 1, full))
        finish()


def _flash_attention(qT_hm, k_hm, vT_hm, sg, lam_p, subln_g_col, *, tq, lam_init):
    n_seq, n_heads, seq_len, _ = k_hm.shape
    tk = tq // 2
    assert seq_len % tq == 0 and tk % CHUNK == 0 and tk % 128 == 0
    row_spec = pl.BlockSpec((1, tq, V_DIM), lambda b, h, i: (b, i, h))
    return pl.pallas_call(
        functools.partial(_flash_kernel, tq=tq, tk=tk, n_q=seq_len // tq, lam_init=lam_init),
        grid=(n_seq, n_heads, seq_len // tq),
        in_specs=[
            pl.BlockSpec((1, 1, V_DIM, seq_len), lambda b, h, i: (b, h, 0, 0)),
            pl.BlockSpec((1, 1, seq_len, V_DIM), lambda b, h, i: (b, h, 0, 0)),
            pl.BlockSpec((1, 1, VT_ROWS, seq_len), lambda b, h, i: (b, h, 0, 0)),
            row_spec,
            pl.BlockSpec(lam_p.shape, lambda b, h, i: (0, 0)),
            pl.BlockSpec(subln_g_col.shape, lambda b, h, i: (0, 0)),
        ],
        out_specs=row_spec,
        out_shape=jax.ShapeDtypeStruct(sg.shape, _BF16),
        scratch_shapes=[pltpu.VMEM((2, V_DIM, tq), _BF16),
                        pltpu.VMEM((2, tk, tq), _F32), pltpu.VMEM((2, tk, tq), _F32),
                        pltpu.VMEM((2, 8, tq), _F32), pltpu.VMEM((2, 8, tq), _F32),
                        pltpu.VMEM((2, 8, tq), _F32),
                        pltpu.VMEM((2, VT_ROWS // 8, 8, tq), _F32),
                        pltpu.VMEM((tk, tq // 2), _F32)],
        compiler_params=pltpu.CompilerParams(
            dimension_semantics=("arbitrary", "arbitrary", "arbitrary"),
            vmem_limit_bytes=V7X_VMEM_LIMIT_BYTES),
        name="flash_diff_attn",
    )(qT_hm, k_hm, vT_hm, sg, lam_p, subln_g_col)


def _decode_attn_kernel(q_ref, k_ref, v_ref, ck_ref, cv_ref, sg_ref, lam_ref, g_ref, o_ref,
                        *, n_heads, vis_past, vis_new, lam_init):
    lam = _lambda_full(lam_ref, lam_init)
    for h in range(n_heads):
        cs = slice(h * V_DIM, (h + 1) * V_DIM)
        q2 = jnp.concatenate(_split_maps(q_ref[0, h]), axis=0)
        k_new, v_new = k_ref[0, h], v_ref[0, h]
        k_past = ck_ref[0, h].astype(_BF16)
        v_past = cv_ref[0, h].astype(_BF16)
        s_past = _dot_nt(q2, k_past)
        s_new = _dot_nt(q2, k_new)
        if vis_past is not None:
            s_past = jnp.where(jnp.concatenate([vis_past, vis_past], axis=0), s_past, -jnp.inf)
        if vis_new is not None:
            s_new = jnp.where(jnp.concatenate([vis_new, vis_new], axis=0), s_new, -jnp.inf)
        m = jnp.maximum(jnp.max(s_past, axis=1, keepdims=True), jnp.max(s_new, axis=1, keepdims=True))
        p_past = jnp.exp2(s_past - m)
        p_new = jnp.exp2(s_new - m)
        l = jnp.sum(p_past, axis=1, keepdims=True) + jnp.sum(p_new, axis=1, keepdims=True)
        o = _dot(p_past.astype(_BF16), v_past) + _dot(p_new.astype(_BF16), v_new)
        t = o.shape[0] // 2
        o_ref[0, :, cs] = _finish_heads(o[:t], l[:t], o[t:], l[t:], lam, lam_init, g_ref[...],
                                        sg_ref[0, :, cs])


def _static_visibility(q_pos, k_pos):
    vis = (k_pos[None, :] // CHUNK) <= (q_pos[:, None] // CHUNK)
    return None if vis.all() else jnp.asarray(vis)


def _decode_attention(q_hm, k_hm, v_hm, ck, cv, sg, lam_p, subln_g, *, lam_init):
    n_seq, n_heads, t_new, _ = q_hm.shape
    past = ck.shape[2]
    q_pos = past + np.arange(t_new)
    vis_past = _static_visibility(q_pos, np.arange(past))
    vis_new = _static_visibility(q_pos, q_pos)
    new_spec = pl.BlockSpec((1, n_heads, t_new, V_DIM), lambda b: (b, 0, 0, 0))
    cache_spec = pl.BlockSpec((1, n_heads, past, V_DIM), lambda b: (b, 0, 0, 0))
    row_spec = pl.BlockSpec((1, t_new, n_heads * V_DIM), lambda b: (b, 0, 0))
    return pl.pallas_call(
        functools.partial(_decode_attn_kernel, n_heads=n_heads, vis_past=vis_past, vis_new=vis_new,
                          lam_init=lam_init),
        grid=(n_seq,),
        in_specs=[new_spec, new_spec, new_spec, cache_spec, cache_spec, row_spec,
                  pl.BlockSpec(lam_p.shape, lambda b: (0, 0)),
                  pl.BlockSpec(subln_g.shape, lambda b: (0, 0))],
        out_specs=row_spec,
        out_shape=jax.ShapeDtypeStruct(sg.shape, _BF16),
        compiler_params=pltpu.CompilerParams(
            dimension_semantics=("arbitrary",), vmem_limit_bytes=V7X_VMEM_LIMIT_BYTES),
        name="decode_diff_attn",
    )(q_hm, k_hm, v_hm, ck, cv, sg, lam_p, subln_g)


def _out_kernel(x_ref, mp_ref, ma_ref, w_ref, g_ref, b_ref, y_ref, *, pool_width, alpha, chunk):
    for r0 in range(0, x_ref.shape[0], chunk):
        rs = slice(r0, r0 + chunk)
        out = _dot(mp_ref[rs, :], w_ref[0:pool_width, :]) + _dot(ma_ref[rs, :], w_ref[pool_width:, :])
        z = alpha * x_ref[rs, :] + out
        mu = jnp.mean(z, axis=-1, keepdims=True)
        zc = z - mu
        var = jnp.mean(zc * zc, axis=-1, keepdims=True)
        y_ref[rs, :] = zc * lax.rsqrt(var + LN_EPS) * g_ref[...] + b_ref[...]


def _out_proj(x2d, mp, ma, w_out_b, ln_g, ln_b, *, tm, alpha):
    n_rows, d_model = x2d.shape
    pw, aw = mp.shape[1], ma.shape[1]
    assert n_rows % tm == 0
    row = lambda i: (i, 0)
    const = lambda i: (0, 0)
    return pl.pallas_call(
        functools.partial(_out_kernel, pool_width=pw, alpha=alpha, chunk=min(tm, OUT_ROW_CHUNK)),
        grid=(n_rows // tm,),
        in_specs=[pl.BlockSpec((tm, d_model), row), pl.BlockSpec((tm, pw), row),
                  pl.BlockSpec((tm, aw), row), pl.BlockSpec(w_out_b.shape, const),
                  pl.BlockSpec((1, d_model), const), pl.BlockSpec((1, d_model), const)],
        out_specs=pl.BlockSpec((tm, d_model), row),
        out_shape=jax.ShapeDtypeStruct((n_rows, d_model), _F32),
        compiler_params=pltpu.CompilerParams(
            dimension_semantics=("arbitrary",), vmem_limit_bytes=V7X_VMEM_LIMIT_BYTES),
        name="out_proj_ln",
    )(x2d, mp, ma, w_out_b, ln_g, ln_b)


def _block_diag(pool_w):
    g, c, d = pool_w.shape
    out = jnp.zeros((g * c, g * d), pool_w.dtype)
    for i in range(g):
        out = out.at[i * c:(i + 1) * c, i * d:(i + 1) * d].set(pool_w[i])
    return out


def _swap_seq_head(rows):
    return jnp.transpose(rows, (0, 2, 1, 3))


def _layer(x, hist, n_valid, k_past, v_past, lam_init, alpha, params, *, rows, tq, tm_out):
    (w_in, pool_w, pool_b, pool_scale, lq1, lk1, lq2, lk2, subln_g, w_out, ln_g, ln_b) = params
    n_seq, seq_len, d_model = x.shape
    pw = pool_scale.shape[0]
    aw = w_out.shape[0] - pw
    n_heads = aw // V_DIM
    x2d = x.reshape(n_seq * seq_len, d_model)
    w_in_b, w_out_b = w_in.astype(_BF16), w_out.astype(_BF16)
    pw_bd = _block_diag(pool_w).astype(_BF16)
    lam_p = jnp.stack([lq1, lk1, lq2, lk2]).astype(_F32)
    prompt = k_past is None
    nb = 1 if prompt else n_seq
    krow, vrow, q_hm, k_hm, v_hm, sg, mp, pst = _in_proj(
        x2d, hist, w_in_b, pw_bd, pool_b.reshape(1, pw).astype(_F32),
        pool_scale.reshape(1, pw).astype(_F32),
        n_seq=n_seq, seq_len=seq_len, nb=nb, rows=rows, n_valid=n_valid, transposed=prompt)
    sg3 = sg.reshape(n_seq, seq_len, aw)
    if prompt:
        ma = _flash_attention(q_hm, k_hm, v_hm, sg3, lam_p, subln_g.reshape(V_DIM, 1).astype(_F32),
                              tq=tq, lam_init=lam_init)
    else:
        ma = _decode_attention(q_hm, k_hm, v_hm, _swap_seq_head(k_past), _swap_seq_head(v_past), sg3,
                               lam_p, subln_g.reshape(1, V_DIM).astype(_F32), lam_init=lam_init)
    y = _out_proj(x2d, mp, ma.reshape(n_seq * seq_len, aw), w_out_b,
                  ln_g.reshape(1, d_model).astype(_F32), ln_b.reshape(1, d_model).astype(_F32),
                  tm=tm_out, alpha=alpha)
    return y.reshape(n_seq, seq_len, d_model), _swap_seq_head(krow), _swap_seq_head(vrow), pst


def kernel(x_prompt, x_sample, cache_k, cache_v, state_pool, w_in, pool_w, pool_b, pool_scale,
           lambda_q1, lambda_k1, lambda_q2, lambda_k2, subln_g, w_out, ln_g, ln_b):
    depth = w_in.shape[0]
    pw = pool_scale.shape[1]
    pool_hist = max(POOL_WINDOWS) - 1
    alpha = (2.0 * depth) ** 0.25
    n_valid_sample = min(cache_k.shape[2], pool_hist)
    yp, ys = x_prompt, x_sample
    outs = [[] for _ in range(6)]
    for l in range(depth):
        lam_init = _lambda_init(l)
        params = (w_in[l], pool_w[l], pool_b[l], pool_scale[l], lambda_q1[l], lambda_k1[l],
                  lambda_q2[l], lambda_k2[l], subln_g[l], w_out[l], ln_g[l], ln_b[l])
        zero_hist = jnp.zeros((yp.shape[0], HIST_ROWS, pw), _F32)
        yp, kp, vp, pp = _layer(yp, zero_hist, 0, None, None, lam_init, alpha, params,
                                rows=1024, tq=1024, tm_out=2048)
        hist_s = jnp.pad(state_pool[l].astype(_F32), ((0, 0), (HIST_ROWS - pool_hist, 0), (0, 0)))
        ys, kn, vn, pn = _layer(ys, hist_s, n_valid_sample, cache_k[l], cache_v[l], lam_init, alpha,
                                params, rows=ys.shape[1], tq=None,
                                tm_out=ys.shape[0] * ys.shape[1])
        for lst, val in zip(outs, (kp, vp, pp[:, HIST_ROWS - pool_hist:], kn, vn,
                                   pn[:, HIST_ROWS - pool_hist:])):
            lst.append(val)
    return (yp, ys) + tuple(o[0][None] if depth == 1 else jnp.stack(o) for o in outs)
```
